```python
import jax, jax.numpy as jnp
from jax import lax
import numpy as np

D_MODEL = 1024
BATCH = 4
SEQ = 8192
DEPTH = 1

D_SSM = D_MODEL
SSM_HEAD_DIM = 64
SSM_HEADS = D_SSM // SSM_HEAD_DIM
SSM_GROUPS = 2
SSM_STATE = 128
CONV_WIDTH = 5
CHUNK = 128
D_XBC = D_SSM + 2 * SSM_GROUPS * SSM_STATE

ATTN_HEAD_DIM = 64
ATTN_HEADS = D_MODEL // ATTN_HEAD_DIM
ATTN_KV_HEADS = 4
D_ATTN = ATTN_HEADS * ATTN_HEAD_DIM
D_KV = ATTN_KV_HEADS * ATTN_HEAD_DIM
WINDOW = 128
BLOCK = 128

D_MIX = D_SSM + D_ATTN
D_FF = 4 * D_MODEL
EPS = 1e-6

IN_SIZES = (D_SSM, D_XBC, 2 * SSM_HEADS, D_ATTN, D_KV, D_KV)
D_IN = sum(IN_SIZES)
IN_OFFSETS = tuple(int(o) for o in np.cumsum(IN_SIZES)[:-1])

kernel_name = "hybrid_ssd_swa_encoder_layer"


def rmsnorm(x, w):
    xf = x.astype(jnp.float32)
    y = xf * lax.rsqrt(jnp.mean(xf * xf, axis=-1, keepdims=True) + EPS)
    return (y * w.astype(jnp.float32)).astype(x.dtype)


def depthwise_conv_centred(u, w, b):
    c = u.shape[-1]
    out = lax.conv_general_dilated(
        u, w[:, None, :].astype(u.dtype), window_strides=(1,),
        padding=[(CONV_WIDTH // 2, CONV_WIDTH // 2)],
        dimension_numbers=("NWC", "WIO", "NWC"), feature_group_count=c)
    return out + b.astype(u.dtype)


def ssd_scan(X, dt, A, B, C):
    b, l, h, p = X.shape
    g, n = B.shape[2], B.shape[3]
    r = h // g
    c = l // CHUNK
    Xd = (X.astype(jnp.float32) * dt[..., None]).reshape(b, c, CHUNK, g, r, p)
    dA = (dt * A).reshape(b, c, CHUNK, g, r).transpose(0, 3, 4, 1, 2)
    Bc = B.astype(jnp.float32).reshape(b, c, CHUNK, g, n)
    Cc = C.astype(jnp.float32).reshape(b, c, CHUNK, g, n)
    A_cs = jnp.cumsum(dA, axis=-1)
    tri = jnp.tril(jnp.ones((CHUNK, CHUNK), dtype=bool))
    Lmat = jnp.exp(jnp.where(tri, A_cs[..., :, None] - A_cs[..., None, :], -jnp.inf))
    CB = jnp.einsum("bclgn,bcsgn->bgcls", Cc, Bc)
    Y_diag = jnp.einsum("bgrcls,bcsgrp->bclgrp", CB[:, :, None] * Lmat, Xd)
    decay_states = jnp.exp(A_cs[..., -1:] - A_cs)
    states = jnp.einsum("bcsgn,bgrcs,bcsgrp->bcgrpn", Bc, decay_states, Xd)
    chunk_decay = jnp.exp(A_cs[..., -1])

    def step(carry, inp):
        s_c, dec_c = inp
        return carry * dec_c[..., None, None] + s_c, carry

    init = jnp.zeros_like(states[:, 0])
    _, prev = lax.scan(step, init, (states.transpose(1, 0, 2, 3, 4, 5),
                                    chunk_decay.transpose(3, 0, 1, 2)))
    Y_off = jnp.einsum("bclgn,cbgrpn,bgrcl->bclgrp", Cc, prev, jnp.exp(A_cs))
    return (Y_diag + Y_off).reshape(b, l, h, p)


def ssd_mixer(z, xbc, dt_raw, conv_w, conv_b, dt_bias, a_log, d_skip, norm_w):
    b, l, _ = xbc.shape
    xbc = jax.nn.silu(depthwise_conv_centred(xbc, conv_w, conv_b))
    xs, Bm, Cm = jnp.split(xbc, (D_SSM, D_SSM + SSM_GROUPS * SSM_STATE), axis=-1)
    X = xs.reshape(b, l, SSM_HEADS, SSM_HEAD_DIM)
    Bm = Bm.reshape(b, l, SSM_GROUPS, SSM_STATE)
    Cm = Cm.reshape(b, l, SSM_GROUPS, SSM_STATE)
    dt = jax.nn.softplus(dt_raw.astype(jnp.float32).reshape(b, l, 2, SSM_HEADS)
                         + dt_bias.astype(jnp.float32))
    A = -jnp.exp(a_log.astype(jnp.float32))
    flip = lambda t: jnp.flip(t, axis=1)
    y_f = ssd_scan(X, dt[:, :, 0], A[0], Bm, Cm)
    y_b = flip(ssd_scan(flip(X), flip(dt[:, :, 1]), A[1], flip(Bm), flip(Cm)))
    y = y_f + y_b + X.astype(jnp.float32) * d_skip.astype(jnp.float32)[:, None]
    y = y.reshape(b, l, D_SSM)
    return rmsnorm(y * jax.nn.silu(z.astype(jnp.float32)), norm_w)


def window_attention(q, k, v, sink):
    b, l, _ = q.shape
    nb = l // BLOCK
    G, R, Dh = ATTN_KV_HEADS, ATTN_HEADS // ATTN_KV_HEADS, ATTN_HEAD_DIM
    qb = q.reshape(b, nb, BLOCK, G, R, Dh)
    pad = ((0, 0), (WINDOW, WINDOW), (0, 0))
    kp = jnp.pad(k, pad).reshape(b, nb + 2, BLOCK, G, Dh)
    vp = jnp.pad(v, pad).reshape(b, nb + 2, BLOCK, G, Dh)
    kb = jnp.concatenate([kp[:, :-2], kp[:, 1:-1], kp[:, 2:]], axis=2)
    vb = jnp.concatenate([vp[:, :-2], vp[:, 1:-1], vp[:, 2:]], axis=2)
    s = jnp.einsum("bnqgrd,bnkgd->bgrnqk", qb, kb).astype(jnp.float32) * (Dh ** -0.5)
    qi = jnp.arange(BLOCK)[:, None]
    kj = jnp.arange(3 * BLOCK)[None, :]
    dist = kj - BLOCK - qi
    key_abs = jnp.arange(nb)[:, None, None] * BLOCK + (kj - BLOCK)[None]
    valid = (jnp.abs(dist) <= WINDOW)[None] & (key_abs >= 0) & (key_abs < l)
    slopes = 2.0 ** (-8.0 * jnp.arange(1, ATTN_HEADS + 1, dtype=jnp.float32) / ATTN_HEADS)
    slopes = slopes.reshape(G, R)[:, :, None, None, None]
    s = jnp.where(valid, s - slopes * jnp.abs(dist).astype(jnp.float32), -jnp.inf)
    sk = sink.astype(jnp.float32).reshape(G, R)[:, :, None, None, None]
    m = jnp.maximum(jnp.max(s, axis=-1, keepdims=True), sk)
    p = jnp.exp(s - m)
    probs = p / (jnp.sum(p, axis=-1, keepdims=True) + jnp.exp(sk - m))
    o = jnp.einsum("bgrnqk,bnkgd->bnqgrd", probs.astype(v.dtype), vb)
    return o.reshape(b, l, D_ATTN)


def setup_inputs(seed: int = 0) -> dict:
    key = jax.random.key(seed)
    ks = jax.random.split(key, 18)
    f32 = jnp.float32
    nrm = lambda k, shape, scale: jax.random.normal(k, shape, f32) * scale
    gain = lambda k, shape: 1.0 + 0.05 * jax.random.normal(k, shape, f32)
    dt0 = jnp.exp(jax.random.uniform(ks[5], (DEPTH, 2, SSM_HEADS), f32,
                                     jnp.log(1e-3), jnp.log(1e-1)))
    dt_bias = dt0 + jnp.log(-jnp.expm1(-dt0))
    a_log = jnp.log(jax.random.uniform(ks[6], (DEPTH, 2, SSM_HEADS), f32, 1.0, 16.0))
    return {
        "x": nrm(ks[0], (BATCH, SEQ, D_MODEL), 1.0),
        "norm_mix_pre": gain(ks[1], (DEPTH, D_MODEL)),
        "w_in": nrm(ks[2], (DEPTH, D_MODEL, D_IN), D_MODEL ** -0.5),
        "conv_w": nrm(ks[3], (DEPTH, CONV_WIDTH, D_XBC), CONV_WIDTH ** -0.5),
        "conv_b": nrm(ks[4], (DEPTH, D_XBC), 0.01),
        "dt_bias": dt_bias,
        "a_log": a_log,
        "d_skip": gain(ks[7], (DEPTH, SSM_HEADS)),
        "ssm_norm": gain(ks[8], (DEPTH, D_SSM)),
        "attn_sink": nrm(ks[9], (DEPTH, ATTN_HEADS), 0.5),
        "w_out": nrm(ks[10], (DEPTH, D_MIX, D_MODEL), D_MIX ** -0.5),
        "norm_mix_post": gain(ks[11], (DEPTH, D_MODEL)),
        "norm_mlp_pre": gain(ks[12], (DEPTH, D_MODEL)),
        "w_up": nrm(ks[13], (DEPTH, D_MODEL, D_FF), D_MODEL ** -0.5),
        "w_down": nrm(ks[14], (DEPTH, D_FF, D_MODEL), D_FF ** -0.5),
        "norm_mlp_post": gain(ks[15], (DEPTH, D_MODEL)),
    }


def reference(x, norm_mix_pre, w_in, conv_w, conv_b, dt_bias, a_log, d_skip, ssm_norm,
              attn_sink, w_out, norm_mix_post, norm_mlp_pre, w_up, w_down, norm_mlp_post):
    for i in range(DEPTH):
        h = rmsnorm(x, norm_mix_pre[i])
        proj = h @ w_in[i]
        z, xbc, dt_raw, q, k, v = jnp.split(proj, IN_OFFSETS, axis=-1)
        y_ssm = ssd_mixer(z, xbc, dt_raw, conv_w[i], conv_b[i], dt_bias[i], a_log[i],
                          d_skip[i], ssm_norm[i])
        y_attn = window_attention(q, k, v, attn_sink[i])
        mix = jnp.concatenate([y_ssm.astype(x.dtype), y_attn.astype(x.dtype)], axis=-1) @ w_out[i]
        x = x + rmsnorm(mix, norm_mix_post[i])
        h = rmsnorm(x, norm_mlp_pre[i])
        f = jnp.square(jax.nn.relu(h @ w_up[i])) @ w_down[i]
        x = x + rmsnorm(f, norm_mlp_post[i])
    return x
```

```python
import functools
import math

import jax
import jax.numpy as jnp
import numpy as np
from jax import lax
from jax.experimental import pallas as pl
from jax.experimental.pallas import tpu as pltpu

F32 = jnp.float32
BF16 = jnp.bfloat16

EPS = 1e-6
SSM_HEADS = 16
SSM_HEAD_DIM = 64
SSM_GROUPS = 2
SSM_STATE = 128
CONV_WIDTH = 5
CHUNK = 128
ATTN_HEADS = 16
ATTN_KV_HEADS = 4
ATTN_HEAD_DIM = 64
WINDOW = 128
MASKED = -1e30
VMEM_LIMIT = 56 * 1024 * 1024

NT_DIMS = (((1,), (1,)), ((), ()))
TN_DIMS = (((0,), (0,)), ((), ()))


def _rmsnorm(x, w):
    return x * lax.rsqrt(jnp.mean(x * x, axis=-1, keepdims=True) + EPS) * w


def _params(*semantics):
    return pltpu.CompilerParams(dimension_semantics=semantics, vmem_limit_bytes=VMEM_LIMIT)


def _resident(shape):
    return pl.BlockSpec(shape, lambda *_: (0,) * len(shape), pipeline_mode=pl.Buffered(1))


def _in_proj_body(x_ref, g_ref, w_ref, z_ref, xbc_ref, q_ref, kv_ref, dt_ref, *, d_ssm, d_xbc, d_attn, d_kv2):
    h = _rmsnorm(x_ref[...], g_ref[...]).astype(BF16)

    def seg(lo, width):
        return jnp.dot(h, w_ref[:, lo:lo + width], preferred_element_type=F32)

    o = 0
    z_ref[...] = seg(o, d_ssm).astype(BF16)
    o += d_ssm
    xbc_ref[...] = seg(o, d_xbc).astype(BF16)
    o += d_xbc
    q_ref[...] = (seg(o, d_attn) * (ATTN_HEAD_DIM ** -0.5)).astype(BF16)
    o += d_attn
    kv_ref[...] = seg(o, d_kv2).astype(BF16)
    o += d_kv2
    dt_ref[...] = seg(o, 128)


def _in_proj(x2, gain, w, *, tm, d_ssm, d_xbc, d_attn, d_kv2):
    t, d = x2.shape
    n = w.shape[1]
    row = lambda width: pl.BlockSpec((tm, width), lambda i: (i, 0))
    return pl.pallas_call(
        functools.partial(_in_proj_body, d_ssm=d_ssm, d_xbc=d_xbc, d_attn=d_attn, d_kv2=d_kv2),
        grid=(t // tm,),
        in_specs=[row(d), _resident((1, d)), _resident((d, n))],
        out_specs=[row(d_ssm), row(d_xbc), row(d_attn), row(d_kv2), row(128)],
        out_shape=[jax.ShapeDtypeStruct((t, d_ssm), BF16), jax.ShapeDtypeStruct((t, d_xbc), BF16),
                   jax.ShapeDtypeStruct((t, d_attn), BF16), jax.ShapeDtypeStruct((t, d_kv2), BF16),
                   jax.ShapeDtypeStruct((t, 128), F32)],
        compiler_params=_params("parallel"),
        name="in_proj",
    )(x2, gain, w)


def _split3(v):
    hi = v.astype(BF16)
    r = v - hi.astype(F32)
    mid = r.astype(BF16)
    lo = (r - mid.astype(F32)).astype(BF16)
    return hi, mid, lo


def _ssd_prep_body(cur_ref, prev_ref, next_ref, dtraw_ref, cw_ref, cb_ref, dtb_ref, alog_ref, tri_ref,
                   xc_ref, bc_ref, dcol_ref, drow_ref, *, d_ssm):
    c = pl.program_id(1)
    nc = pl.num_programs(1)
    cur = cur_ref[...].astype(F32)
    prev = jnp.where(c > 0, prev_ref[...].astype(F32)[8:16], 0.0)
    nxt = jnp.where(c < nc - 1, next_ref[...].astype(F32)[0:8], 0.0)
    ext = jnp.concatenate([prev, cur, nxt], axis=0)
    acc = cb_ref[...] + cw_ref[0:1, :] * ext[6:6 + CHUNK]
    for j in range(1, CONV_WIDTH):
        acc = acc + cw_ref[j:j + 1, :] * ext[6 + j:6 + j + CHUNK]
    act = acc / (1.0 + jnp.exp(-acc))
    xc_ref[...] = act[:, :d_ssm].astype(BF16)
    bc_ref[...] = act[:, d_ssm:].astype(BF16)

    v = dtraw_ref[...] + dtb_ref[...]
    dt = jnp.maximum(v, 0.0) + jnp.log1p(jnp.exp(-jnp.abs(v)))
    da = dt * (-jnp.exp(alog_ref[...]))
    tri = tri_ref[...]
    hi, mid, lo = _split3(da)
    pre = (jnp.dot(tri, hi, preferred_element_type=F32) + jnp.dot(tri, mid, preferred_element_type=F32)
           + jnp.dot(tri, lo, preferred_element_type=F32))
    total = pre[CHUNK - 1:CHUNK, :]
    lane = lax.broadcasted_iota(jnp.int32, da.shape, 1)
    a = jnp.where(lane < SSM_HEADS, pre, total - pre + da)
    ea = jnp.exp(a)
    w = jnp.exp(total - a) * dt
    g = a - jnp.log(dt)
    dcol_ref[...] = jnp.where(lane < 32, a,
                              jnp.where(lane < 64, pltpu.roll(ea, 32, 1),
                                        jnp.where(lane < 96, pltpu.roll(w, 64, 1), 0.0)))
    drow_ref[...] = g.T[0:2 * SSM_HEADS, :]


def _ssd_prep(xbc, dt_raw, conv_w, conv_b, dt_bias, a_log, *, batch, seq, d_ssm):
    t, d_xbc = xbc.shape
    nc = seq // CHUNK
    sub = CHUNK // 16
    nblk16 = t // 16
    tri = jnp.tril(jnp.ones((CHUNK, CHUNK), F32)).astype(BF16)
    chunk = lambda width: pl.BlockSpec((CHUNK, width), lambda b, c: (b * nc + c, 0))
    return pl.pallas_call(
        functools.partial(_ssd_prep_body, d_ssm=d_ssm),
        grid=(batch, nc),
        in_specs=[
            chunk(d_xbc),
            pl.BlockSpec((16, d_xbc), lambda b, c: (jnp.maximum((b * nc + c) * sub - 1, 0), 0)),
            pl.BlockSpec((16, d_xbc), lambda b, c: (jnp.minimum((b * nc + c + 1) * sub, nblk16 - 1), 0)),
            chunk(128),
            _resident((CONV_WIDTH, d_xbc)), _resident((1, d_xbc)),
            _resident((1, 128)), _resident((1, 128)), _resident((CHUNK, CHUNK)),
        ],
        out_specs=[chunk(d_ssm), chunk(d_xbc - d_ssm), chunk(128),
                   pl.BlockSpec((2 * SSM_HEADS, CHUNK), lambda b, c: (b * nc + c, 0))],
        out_shape=[jax.ShapeDtypeStruct((t, d_ssm), BF16), jax.ShapeDtypeStruct((t, d_xbc - d_ssm), BF16),
                   jax.ShapeDtypeStruct((t, 128), F32),
                   jax.ShapeDtypeStruct((batch * nc * 2 * SSM_HEADS, CHUNK), F32)],
        compiler_params=_params("parallel", "parallel"),
        name="ssd_prep",
    )(xbc, xbc, xbc, dt_raw, conv_w, conv_b, dt_bias, a_log, tri)


def _ssd_scan_body(xf_ref, bcf_ref, dcf_ref, drf_ref, xb_ref, bcb_ref, dcb_ref, dskip_ref,
                   yf_ref, yb_ref, stf_ref, stb_ref):
    @pl.when(pl.program_id(1) == 0)
    def _():
        stf_ref[...] = jnp.zeros_like(stf_ref)
        stb_ref[...] = jnp.zeros_like(stb_ref)

    gs = SSM_GROUPS * SSM_STATE
    gw = (SSM_HEADS // SSM_GROUPS) * SSM_HEAD_DIM
    pairs_per_group = gw // 128
    row = lax.broadcasted_iota(jnp.int32, (CHUNK, CHUNK), 0)
    lane = lax.broadcasted_iota(jnp.int32, (CHUNK, CHUNK), 1)
    first_head = lane < SSM_HEAD_DIM
    causal = row >= lane
    anti = row <= lane

    def col(d, j):
        return jnp.broadcast_to(d[:, j:j + 1], (CHUNK, CHUNK))

    def pair_cols(d, j):
        return jnp.where(first_head, col(d, j), col(d, j + 1))

    def visit(x_ref, bc_ref, dc, st_ref, y_ref, direction, dr):
        off = direction * SSM_HEADS
        for g in range(SSM_GROUPS):
            bg = bc_ref[:, g * SSM_STATE:(g + 1) * SSM_STATE]
            cg = bc_ref[:, gs + g * SSM_STATE:gs + (g + 1) * SSM_STATE]
            st = st_ref[:, g * gw:(g + 1) * gw]
            y_off = jnp.dot(cg, st.astype(BF16), preferred_element_type=F32)
            if dr is not None:
                cb = lax.dot_general(cg, bg, NT_DIMS, preferred_element_type=F32)
            wx, dec = [], []
            for jj in range(pairs_per_group):
                j = g * pairs_per_group + jj
                h1 = 2 * j
                sl = slice(j * 128, (j + 1) * 128)
                xp = x_ref[:, sl].astype(F32)
                ea = pair_cols(dc, 32 + off + h1)
                y = ea * y_off[:, jj * 128:(jj + 1) * 128]
                if dr is not None:
                    def decay(h):
                        mf = jnp.where(causal, jnp.exp(col(dc, h) - dr[h:h + 1, :]), 0.0)
                        mb = jnp.where(anti, jnp.exp(col(dc, SSM_HEADS + h) - dr[SSM_HEADS + h:SSM_HEADS + h + 1, :]), 0.0)
                        return (cb * (mf + mb)).astype(BF16)
                    lhs = jnp.concatenate([decay(h1), decay(h1 + 1)], axis=1)
                    rhs = jnp.concatenate([jnp.where(first_head, xp, 0.0),
                                           jnp.where(first_head, 0.0, xp)], axis=0).astype(BF16)
                    y = y + jnp.dot(lhs, rhs, preferred_element_type=F32) + xp * dskip_ref[:, sl]
                y_ref[:, sl] = y.astype(BF16)
                wx.append((pair_cols(dc, 64 + off + h1) * xp).astype(BF16))
                edge = CHUNK - 1 if direction == 0 else 0
                dec.append(ea[edge:edge + 1, :])
            wx = jnp.concatenate(wx, axis=1)
            dec = jnp.concatenate(dec, axis=1)
            st_ref[:, g * gw:(g + 1) * gw] = st * dec + lax.dot_general(bg, wx, TN_DIMS, preferred_element_type=F32)

    visit(xf_ref, bcf_ref, dcf_ref[...], stf_ref, yf_ref, 0, drf_ref[...])
    visit(xb_ref, bcb_ref, dcb_ref[...], stb_ref, yb_ref, 1, None)


def _ssd_scan(xc, bc, dcol, drow, dskip, *, batch, seq):
    t, d_ssm = xc.shape
    nc = seq // CHUNK
    fwd = lambda rows, width: pl.BlockSpec((rows, width), lambda b, c: (b * nc + c, 0))
    bwd = lambda rows, width: pl.BlockSpec((rows, width), lambda b, c: (b * nc + nc - 1 - c, 0))
    return pl.pallas_call(
        _ssd_scan_body,
        grid=(batch, nc),
        in_specs=[fwd(CHUNK, d_ssm), fwd(CHUNK, bc.shape[1]), fwd(CHUNK, 128), fwd(2 * SSM_HEADS, CHUNK),
                  bwd(CHUNK, d_ssm), bwd(CHUNK, bc.shape[1]), bwd(CHUNK, 128), _resident((1, d_ssm))],
        out_specs=[fwd(CHUNK, d_ssm), bwd(CHUNK, d_ssm)],
        out_shape=[jax.ShapeDtypeStruct((t, d_ssm), BF16), jax.ShapeDtypeStruct((t, d_ssm), BF16)],
        scratch_shapes=[pltpu.VMEM((SSM_STATE, d_ssm), F32), pltpu.VMEM((SSM_STATE, d_ssm), F32)],
        compiler_params=_params("arbitrary", "arbitrary"),
        name="ssd_scan",
    )(xc, bc, dcol, drow, xc, bc, dcol, dskip)


def _attn_body(q_ref, kvp_ref, kvc_ref, kvn_ref, bias_ref, sink_ref, o_ref):
    rep = ATTN_HEADS // ATTN_KV_HEADS
    d_kv = ATTN_KV_HEADS * ATTN_HEAD_DIM
    kv = jnp.concatenate([kvp_ref[...], kvc_ref[...], kvn_ref[...]], axis=0)
    for g in range(ATTN_KV_HEADS):
        kg = kv[:, g * ATTN_HEAD_DIM:(g + 1) * ATTN_HEAD_DIM]
        vg = kv[:, d_kv + g * ATTN_HEAD_DIM:d_kv + (g + 1) * ATTN_HEAD_DIM]
        qg = jnp.concatenate([q_ref[:, (g * rep + r) * ATTN_HEAD_DIM:(g * rep + r + 1) * ATTN_HEAD_DIM]
                              for r in range(rep)], axis=0)
        s = lax.dot_general(kg, qg, NT_DIMS, preferred_element_type=F32) + bias_ref[0, g]
        sk = sink_ref[g]
        m = jnp.maximum(jnp.max(s, axis=0, keepdims=True), sk)
        p = jnp.exp(s - m)
        denom = jnp.sum(p, axis=0, keepdims=True) + jnp.exp(sk - m)
        o = lax.dot_general(vg, p.astype(BF16), TN_DIMS, preferred_element_type=F32) / denom
        for r in range(0, rep, 2):
            two = jnp.concatenate([o[:, r * WINDOW:(r + 1) * WINDOW], o[:, (r + 1) * WINDOW:(r + 2) * WINDOW]], axis=0)
            lo = (g * rep + r) * ATTN_HEAD_DIM
            o_ref[:, lo:lo + 2 * ATTN_HEAD_DIM] = two.T.astype(BF16)


def _attention(q, kv, bias, sink, *, batch, seq):
    t, d_attn = q.shape
    nb = seq // WINDOW
    rep = ATTN_HEADS // ATTN_KV_HEADS
    blk = lambda f: pl.BlockSpec((WINDOW, kv.shape[1]), f)
    return pl.pallas_call(
        _attn_body,
        grid=(batch, nb),
        in_specs=[
            pl.BlockSpec((WINDOW, d_attn), lambda b, n: (b * nb + n, 0)),
            blk(lambda b, n: (b * nb + jnp.maximum(n - 1, 0), 0)),
            blk(lambda b, n: (b * nb + n, 0)),
            blk(lambda b, n: (b * nb + jnp.minimum(n + 1, nb - 1), 0)),
            pl.BlockSpec((1, ATTN_KV_HEADS, 3 * WINDOW, rep * WINDOW),
                         lambda b, n: (jnp.where(n == 0, 0, jnp.where(n == nb - 1, 2, 1)), 0, 0, 0)),
            _resident((ATTN_KV_HEADS, 1, rep * WINDOW)),
        ],
        out_specs=pl.BlockSpec((WINDOW, d_attn), lambda b, n: (b * nb + n, 0)),
        out_shape=jax.ShapeDtypeStruct((t, d_attn), BF16),
        compiler_params=_params("parallel", "parallel"),
        name="attention",
    )(q, kv, kv, kv, bias, sink)


def _attn_bias_tables():
    rep = ATTN_HEADS // ATTN_KV_HEADS
    kj = np.arange(3 * WINDOW)[:, None]
    qi = np.arange(WINDOW)[None, :]
    dist = np.abs(kj - WINDOW - qi).astype(np.float32)
    in_window = dist <= WINDOW
    slopes = 2.0 ** (-8.0 * np.arange(1, ATTN_HEADS + 1, dtype=np.float32) / ATTN_HEADS)
    alibi = -slopes.reshape(ATTN_KV_HEADS, rep)[:, None, :, None] * dist[None, :, None, :]
    cases = []
    for has_prev, has_next in ((False, True), (True, True), (True, False)):
        ok = in_window & ((kj >= WINDOW) | has_prev) & ((kj < 2 * WINDOW) | has_next)
        cases.append(np.where(ok[None, :, None, :], alibi, np.float32(MASKED)))
    out = np.stack(cases).astype(np.float32)
    return out.reshape(3, ATTN_KV_HEADS, 3 * WINDOW, rep * WINDOW)


def _out_proj_body(x_ref, yf_ref, yb_ref, z_ref, ya_ref, nssm_ref, wo_ref, npost_ref, o_ref, *, d_ssm):
    y = yf_ref[...].astype(F32) + yb_ref[...].astype(F32)
    z = z_ref[...].astype(F32)
    ys = _rmsnorm(y * (z / (1.0 + jnp.exp(-z))), nssm_ref[...]).astype(BF16)
    mix = (jnp.dot(ys, wo_ref[:d_ssm, :], preferred_element_type=F32)
           + jnp.dot(ya_ref[...], wo_ref[d_ssm:, :], preferred_element_type=F32))
    o_ref[...] = x_ref[...] + _rmsnorm(mix, npost_ref[...])


def _out_proj(x2, yf, yb, z, ya, nssm, wo, npost, *, tm):
    t, d = x2.shape
    d_ssm = yf.shape[1]
    row = lambda width: pl.BlockSpec((tm, width), lambda i: (i, 0))
    return pl.pallas_call(
        functools.partial(_out_proj_body, d_ssm=d_ssm),
        grid=(t // tm,),
        in_specs=[row(d), row(d_ssm), row(d_ssm), row(d_ssm), row(ya.shape[1]),
                  _resident((1, d_ssm)), _resident(wo.shape), _resident((1, d))],
        out_specs=row(d),
        out_shape=jax.ShapeDtypeStruct((t, d), F32),
        compiler_params=_params("parallel"),
        name="out_proj",
    )(x2, yf, yb, z, ya, nssm, wo, npost)


def _mlp_body(x_ref, npre_ref, wu_ref, wd_ref, npost_ref, o_ref):
    x = x_ref[...]
    h = _rmsnorm(x, npre_ref[...]).astype(BF16)
    u = jnp.maximum(jnp.dot(h, wu_ref[...], preferred_element_type=F32), 0.0)
    f = jnp.dot((u * u).astype(BF16), wd_ref[...], preferred_element_type=F32)
    o_ref[...] = x + _rmsnorm(f, npost_ref[...])


def _mlp(x2, npre, wu, wd, npost, *, tm):
    t, d = x2.shape
    row = pl.BlockSpec((tm, d), lambda i: (i, 0))
    return pl.pallas_call(
        _mlp_body,
        grid=(t // tm,),
        in_specs=[row, _resident((1, d)), _resident(wu.shape), _resident(wd.shape), _resident((1, d))],
        out_specs=row,
        out_shape=jax.ShapeDtypeStruct((t, d), F32),
        compiler_params=_params("parallel"),
        name="mlp",
    )(x2, npre, wu, wd, npost)


def _layer(x2, p, *, batch, seq):
    d = x2.shape[1]
    d_ssm = SSM_HEADS * SSM_HEAD_DIM
    d_xbc = d_ssm + 2 * SSM_GROUPS * SSM_STATE
    d_attn = ATTN_HEADS * ATTN_HEAD_DIM
    d_kv2 = 2 * ATTN_KV_HEADS * ATTN_HEAD_DIM
    n_dt = 2 * SSM_HEADS
    row = lambda v: v.reshape(1, -1).astype(F32)
    pad_lanes = lambda v: jnp.pad(row(v), ((0, 0), (0, 128 - v.size)))

    w_in = p["w_in"]
    o_dt = d_ssm + d_xbc
    w_in = jnp.concatenate([w_in[:, :o_dt], w_in[:, o_dt + n_dt:], w_in[:, o_dt:o_dt + n_dt],
                            jnp.zeros((d, 128 - n_dt), w_in.dtype)], axis=1).astype(BF16)

    z, xbc, q, kv, dt_raw = _in_proj(x2, row(p["norm_mix_pre"]), w_in, tm=512,
                                     d_ssm=d_ssm, d_xbc=d_xbc, d_attn=d_attn, d_kv2=d_kv2)
    xc, bc, dcol, drow = _ssd_prep(xbc, dt_raw, p["conv_w"].astype(F32), row(p["conv_b"]),
                                   pad_lanes(p["dt_bias"]), pad_lanes(p["a_log"]),
                                   batch=batch, seq=seq, d_ssm=d_ssm)
    dskip = jnp.repeat(p["d_skip"].astype(F32), SSM_HEAD_DIM).reshape(1, d_ssm)
    yf, yb = _ssd_scan(xc, bc, dcol, drow, dskip, batch=batch, seq=seq)

    rep = ATTN_HEADS // ATTN_KV_HEADS
    sink = jnp.repeat(p["attn_sink"].astype(F32), WINDOW).reshape(ATTN_KV_HEADS, 1, rep * WINDOW)
    ya = _attention(q, kv, jnp.asarray(_attn_bias_tables()), sink, batch=batch, seq=seq)

    x2 = _out_proj(x2, yf, yb, z, ya, row(p["ssm_norm"]), p["w_out"].astype(BF16), row(p["norm_mix_post"]), tm=512)
    return _mlp(x2, row(p["norm_mlp_pre"]), p["w_up"].astype(BF16), p["w_down"].astype(BF16),
                row(p["norm_mlp_post"]), tm=512)


def kernel(x, norm_mix_pre, w_in, conv_w, conv_b, dt_bias, a_log, d_skip, ssm_norm, attn_sink, w_out,
           norm_mix_post, norm_mlp_pre, w_up, w_down, norm_mlp_post):
    batch, seq, d = x.shape
    names = ("norm_mix_pre", "w_in", "conv_w", "conv_b", "dt_bias", "a_log", "d_skip", "ssm_norm", "attn_sink",
             "w_out", "norm_mix_post", "norm_mlp_pre", "w_up", "w_down", "norm_mlp_post")
    stacked = (norm_mix_pre, w_in, conv_w, conv_b, dt_bias, a_log, d_skip, ssm_norm, attn_sink, w_out,
               norm_mix_post, norm_mlp_pre, w_up, w_down, norm_mlp_post)
    x2 = x.reshape(batch * seq, d)
    for i in range(w_in.shape[0]):
        x2 = _layer(x2, {k: v[i] for k, v in zip(names, stacked)}, batch=batch, seq=seq)
    return x2.reshape(batch, seq, d)
```

```python
import functools

import jax
import jax.numpy as jnp
import numpy as np
from jax import lax
from jax.experimental import pallas as pl
from jax.experimental.pallas import tpu as pltpu

F32 = jnp.float32
BF16 = jnp.bfloat16

EPS = 1e-6
SSM_HEADS = 16
SSM_HEAD_DIM = 64
SSM_GROUPS = 2
SSM_STATE = 128
CONV_WIDTH = 5
CHUNK = 128
ATTN_HEADS = 16
ATTN_KV_HEADS = 4
ATTN_HEAD_DIM = 64
WINDOW = 128
LANES = 128
SUBLANES = 8
MASKED = -1e30
VMEM_LIMIT = 56 * 1024 * 1024

NT_DIMS = (((1,), (1,)), ((), ()))
TN_DIMS = (((0,), (0,)), ((), ()))


def _rmsnorm(x, w):
    return x * lax.rsqrt(jnp.mean(x * x, axis=-1, keepdims=True) + EPS) * w


def _silu(x):
    return x / (1.0 + jnp.exp(-x))


def _params(*semantics):
    return pltpu.CompilerParams(dimension_semantics=semantics, vmem_limit_bytes=VMEM_LIMIT)


def _resident(shape):
    return pl.BlockSpec(shape, lambda *_: (0,) * len(shape), pipeline_mode=pl.Buffered(1))


def _in_proj_body(x_ref, xprev_ref, xnext_ref, g_ref, w_ref, cw_ref, cb_ref, dtb_ref, alog_ref,
                  z_ref, xc_ref, bt_ref, cc_ref, acol_ref, drow_ref, q_ref, kv_ref,
                  ext_ref, *, tm, blocks_per_seq, d_ssm, d_xbc, d_attn, d_kv2):
    i = pl.program_id(0)
    halo = SUBLANES
    gs = SSM_GROUPS * SSM_STATE
    h = _rmsnorm(x_ref[...], g_ref[...]).astype(BF16)
    h_halo = _rmsnorm(jnp.concatenate([xprev_ref[...], xnext_ref[...]], axis=0), g_ref[...]).astype(BF16)
    he = jnp.concatenate([h, h_halo], axis=0)

    o_xbc, o_q, o_kv, o_dt = d_ssm, d_ssm + d_xbc, d_ssm + d_xbc + d_attn, d_ssm + d_xbc + d_attn + d_kv2
    xbc = jnp.dot(he, w_ref[:, o_xbc:o_q], preferred_element_type=F32)
    first = i % blocks_per_seq == 0
    last = i % blocks_per_seq == blocks_per_seq - 1
    ext_ref[0:halo, :] = jnp.where(first, 0.0, xbc[tm:tm + halo])
    ext_ref[halo:halo + tm, :] = xbc[0:tm]
    ext_ref[halo + tm:, :] = jnp.where(last, 0.0, xbc[tm + halo:])

    z_ref[...] = jnp.dot(h, w_ref[:, 0:o_xbc], preferred_element_type=F32).astype(BF16)
    q_ref[...] = (jnp.dot(h, w_ref[:, o_q:o_kv], preferred_element_type=F32) * (ATTN_HEAD_DIM ** -0.5)).astype(BF16)
    kv_ref[...] = jnp.dot(h, w_ref[:, o_kv:o_dt], preferred_element_type=F32).astype(BF16)
    dt_raw = jnp.dot(h, w_ref[:, o_dt:o_dt + LANES], preferred_element_type=F32)

    lane = lax.broadcasted_iota(jnp.int32, (2 * SSM_HEADS, CHUNK), 1)
    stream = lax.broadcasted_iota(jnp.int32, (2 * SSM_HEADS, CHUNK), 0)
    for k in range(tm // CHUNK):
        r0 = k * CHUNK
        for s in range(d_xbc // LANES):
            c0 = s * LANES
            acc = cb_ref[:, c0:c0 + LANES] + cw_ref[0:1, c0:c0 + LANES] * ext_ref[pl.ds(r0 + halo - 2, CHUNK), c0:c0 + LANES]
            for j in range(1, CONV_WIDTH):
                acc = acc + cw_ref[j:j + 1, c0:c0 + LANES] * ext_ref[pl.ds(r0 + halo - 2 + j, CHUNK), c0:c0 + LANES]
            act = _silu(acc)
            if c0 < d_ssm:
                xc_ref[r0:r0 + CHUNK, c0:c0 + LANES] = act.astype(BF16)
            elif c0 < d_ssm + gs:
                g = (c0 - d_ssm) // SSM_STATE
                bt_ref[(k * SSM_GROUPS + g) * SSM_STATE:(k * SSM_GROUPS + g + 1) * SSM_STATE, :] = act.T.astype(BF16)
            else:
                cc_ref[r0:r0 + CHUNK, c0 - d_ssm - gs:c0 - d_ssm - gs + LANES] = act.astype(BF16)

        v = dt_raw[r0:r0 + CHUNK, :].T[0:2 * SSM_HEADS, :] + dtb_ref[...]
        dt = jnp.maximum(v, 0.0) + jnp.log1p(jnp.exp(-jnp.abs(v)))
        da = dt * (-jnp.exp(alog_ref[...]))
        pre = da
        sh = 1
        while sh < CHUNK:
            pre = pre + jnp.where(lane >= sh, pltpu.roll(pre, sh, 1), 0.0)
            sh *= 2
        total = jnp.broadcast_to(pre[:, CHUNK - 1:CHUNK], pre.shape)
        a = jnp.where(stream < SSM_HEADS, pre, total - pre + da)
        grow = a - jnp.log(dt)
        wrow = jnp.exp(total - a) * dt
        drow_ref[k * 4 * SSM_HEADS:(k + 1) * 4 * SSM_HEADS, :] = jnp.concatenate([grow, wrow], axis=0)
        a_pad = jnp.concatenate([a, jnp.zeros((CHUNK - 2 * SSM_HEADS, CHUNK), F32)], axis=0)
        acol_ref[r0:r0 + CHUNK, :] = a_pad.T


def _in_proj(x2, gain, w, conv_w, conv_b, dt_bias, a_log, *, seq, tm, d_ssm, d_xbc, d_attn, d_kv2):
    t, d = x2.shape
    n = w.shape[1]
    halo = SUBLANES
    nsub = tm // halo
    nhalo = t // halo
    chunks = tm // CHUNK
    gs = SSM_GROUPS * SSM_STATE
    row = lambda rows, width: pl.BlockSpec((rows, width), lambda i: (i, 0))
    out_shape = [
        jax.ShapeDtypeStruct((t, d_ssm), BF16),
        jax.ShapeDtypeStruct((t, d_ssm), BF16),
        jax.ShapeDtypeStruct((t // CHUNK * gs, CHUNK), BF16),
        jax.ShapeDtypeStruct((t, gs), BF16),
        jax.ShapeDtypeStruct((t, LANES), F32),
        jax.ShapeDtypeStruct((t // CHUNK * 4 * SSM_HEADS, CHUNK), F32),
        jax.ShapeDtypeStruct((t, d_attn), BF16),
        jax.ShapeDtypeStruct((t, d_kv2), BF16),
    ]
    return pl.pallas_call(
        functools.partial(_in_proj_body, tm=tm, blocks_per_seq=seq // tm, d_ssm=d_ssm, d_xbc=d_xbc,
                          d_attn=d_attn, d_kv2=d_kv2),
        grid=(t // tm,),
        in_specs=[row(tm, d),
                  pl.BlockSpec((halo, d), lambda i: (jnp.maximum(i * nsub - 1, 0), 0)),
                  pl.BlockSpec((halo, d), lambda i: (jnp.minimum((i + 1) * nsub, nhalo - 1), 0)),
                  _resident((1, d)), _resident((d, n)), _resident((CONV_WIDTH, d_xbc)), _resident((1, d_xbc)),
                  _resident((2 * SSM_HEADS, CHUNK)), _resident((2 * SSM_HEADS, CHUNK))],
        out_specs=[row(tm, d_ssm), row(tm, d_ssm), row(chunks * gs, CHUNK), row(tm, gs), row(tm, LANES),
                   row(chunks * 4 * SSM_HEADS, CHUNK), row(tm, d_attn), row(tm, d_kv2)],
        out_shape=out_shape,
        scratch_shapes=[pltpu.VMEM((tm + 2 * halo, d_xbc), F32)],
        compiler_params=_params("parallel"),
        name="in_proj",
    )(x2, x2, x2, gain, w, conv_w, conv_b, dt_bias, a_log)


def _ssd_scan_body(xf_ref, btf_ref, ccf_ref, acf_ref, drf_ref, xb_ref, btb_ref, ccb_ref, acb_ref, drb_ref,
                   dskip_ref, yf_ref, yb_ref, stf_ref, stb_ref):
    @pl.when(pl.program_id(1) == 0)
    def _():
        stf_ref[...] = jnp.zeros_like(stf_ref)
        stb_ref[...] = jnp.zeros_like(stb_ref)

    gw = (SSM_HEADS // SSM_GROUPS) * SSM_HEAD_DIM
    pairs_per_group = gw // LANES
    row = lax.broadcasted_iota(jnp.int32, (CHUNK, CHUNK), 0)
    lane = lax.broadcasted_iota(jnp.int32, (CHUNK, CHUNK), 1)
    first_head = lane < SSM_HEAD_DIM

    def visit(x_ref, bt_ref, cc_ref, ac, dr, st_ref, y_ref, direction):
        off = direction * SSM_HEADS
        inside = row >= lane if direction == 0 else row <= lane
        edge = CHUNK - 1 if direction == 0 else 0
        for g in range(SSM_GROUPS):
            btg = bt_ref[g * SSM_STATE:(g + 1) * SSM_STATE, :]
            cg = cc_ref[:, g * SSM_STATE:(g + 1) * SSM_STATE]
            cb = jnp.dot(cg, btg, preferred_element_type=F32)
            btf = btg.astype(F32)
            st = st_ref[:, g * gw:(g + 1) * gw]
            y_off = jnp.dot(cg, st.astype(BF16), preferred_element_type=F32)
            for jj in range(pairs_per_group):
                j = g * pairs_per_group + jj
                sl = slice(j * LANES, (j + 1) * LANES)
                decays, weighted_bt, spans = [], [], []
                for h in (2 * j, 2 * j + 1):
                    col = jnp.broadcast_to(ac[:, off + h:off + h + 1], (CHUNK, CHUNK))
                    grow = dr[off + h:off + h + 1, :]
                    wrow = dr[2 * SSM_HEADS + off + h:2 * SSM_HEADS + off + h + 1, :]
                    decays.append((jnp.where(inside, jnp.exp(col - grow), 0.0) * cb).astype(BF16))
                    weighted_bt.append((btf * wrow).astype(BF16))
                    spans.append(jnp.exp(col))
                lhs = jnp.concatenate([jnp.concatenate(decays, axis=1), jnp.concatenate(weighted_bt, axis=1)], axis=0)
                xp = x_ref[:, sl]
                zero = jnp.zeros_like(xp)
                rhs = jnp.concatenate([jnp.where(first_head, xp, zero), jnp.where(first_head, zero, xp)], axis=0)
                out = jnp.dot(lhs, rhs, preferred_element_type=F32)
                ea = jnp.where(first_head, spans[0], spans[1])
                y = out[0:CHUNK] + ea * y_off[:, jj * LANES:(jj + 1) * LANES]
                if direction == 0:
                    y = y + xp.astype(F32) * dskip_ref[:, sl]
                y_ref[:, sl] = y.astype(BF16)
                st_ref[:, sl] = st[:, jj * LANES:(jj + 1) * LANES] * ea[edge:edge + 1, :] + out[CHUNK:]

    visit(xf_ref, btf_ref, ccf_ref, acf_ref[...], drf_ref[...], stf_ref, yf_ref, 0)
    visit(xb_ref, btb_ref, ccb_ref, acb_ref[...], drb_ref[...], stb_ref, yb_ref, 1)


def _ssd_scan(xc, bt, cc, acol, drow, dskip, *, batch, seq):
    t, d_ssm = xc.shape
    nc = seq // CHUNK
    gs = SSM_GROUPS * SSM_STATE
    fwd = lambda rows, width: pl.BlockSpec((rows, width), lambda b, c: (b * nc + c, 0))
    bwd = lambda rows, width: pl.BlockSpec((rows, width), lambda b, c: (b * nc + nc - 1 - c, 0))
    operands = lambda spec: [spec(CHUNK, d_ssm), spec(gs, CHUNK), spec(CHUNK, gs), spec(CHUNK, LANES),
                             spec(4 * SSM_HEADS, CHUNK)]
    return pl.pallas_call(
        _ssd_scan_body,
        grid=(batch, nc),
        in_specs=operands(fwd) + operands(bwd) + [_resident((1, d_ssm))],
        out_specs=[fwd(CHUNK, d_ssm), bwd(CHUNK, d_ssm)],
        out_shape=[jax.ShapeDtypeStruct((t, d_ssm), BF16), jax.ShapeDtypeStruct((t, d_ssm), BF16)],
        scratch_shapes=[pltpu.VMEM((SSM_STATE, d_ssm), F32), pltpu.VMEM((SSM_STATE, d_ssm), F32)],
        compiler_params=_params("arbitrary", "arbitrary"),
        name="ssd_scan",
    )(xc, bt, cc, acol, drow, xc, bt, cc, acol, drow, dskip)


def _attn_body(q_ref, kvp_ref, kvc_ref, kvn_ref, bias_ref, sink_ref, o_ref):
    rep = ATTN_HEADS // ATTN_KV_HEADS
    d_kv = ATTN_KV_HEADS * ATTN_HEAD_DIM
    kv = jnp.concatenate([kvp_ref[...], kvc_ref[...], kvn_ref[...]], axis=0)
    for g in range(ATTN_KV_HEADS):
        kg = kv[:, g * ATTN_HEAD_DIM:(g + 1) * ATTN_HEAD_DIM]
        vg = kv[:, d_kv + g * ATTN_HEAD_DIM:d_kv + (g + 1) * ATTN_HEAD_DIM]
        qg = jnp.concatenate([q_ref[:, (g * rep + r) * ATTN_HEAD_DIM:(g * rep + r + 1) * ATTN_HEAD_DIM]
                              for r in range(rep)], axis=0)
        s = lax.dot_general(kg, qg, NT_DIMS, preferred_element_type=F32) + bias_ref[0, g]
        sk = sink_ref[g]
        m = jnp.maximum(jnp.max(s, axis=0, keepdims=True), sk)
        p = jnp.exp(s - m)
        denom = jnp.sum(p, axis=0, keepdims=True) + jnp.exp(sk - m)
        o = lax.dot_general(vg, p.astype(BF16), TN_DIMS, preferred_element_type=F32) / denom
        for r in range(0, rep, 2):
            two = jnp.concatenate([o[:, r * WINDOW:(r + 1) * WINDOW], o[:, (r + 1) * WINDOW:(r + 2) * WINDOW]], axis=0)
            lo = (g * rep + r) * ATTN_HEAD_DIM
            o_ref[:, lo:lo + 2 * ATTN_HEAD_DIM] = two.T.astype(BF16)


def _attention(q, kv, bias, sink, *, batch, seq):
    t, d_attn = q.shape
    nb = seq // WINDOW
    rep = ATTN_HEADS // ATTN_KV_HEADS
    blk = lambda f: pl.BlockSpec((WINDOW, kv.shape[1]), f)
    return pl.pallas_call(
        _attn_body,
        grid=(batch, nb),
        in_specs=[
            pl.BlockSpec((WINDOW, d_attn), lambda b, n: (b * nb + n, 0)),
            blk(lambda b, n: (b * nb + jnp.maximum(n - 1, 0), 0)),
            blk(lambda b, n: (b * nb + n, 0)),
            blk(lambda b, n: (b * nb + jnp.minimum(n + 1, nb - 1), 0)),
            pl.BlockSpec((1, ATTN_KV_HEADS, 3 * WINDOW, rep * WINDOW),
                         lambda b, n: (jnp.where(n == 0, 0, jnp.where(n == nb - 1, 2, 1)), 0, 0, 0)),
            _resident((ATTN_KV_HEADS, 1, rep * WINDOW)),
        ],
        out_specs=pl.BlockSpec((WINDOW, d_attn), lambda b, n: (b * nb + n, 0)),
        out_shape=jax.ShapeDtypeStruct((t, d_attn), BF16),
        compiler_params=_params("parallel", "parallel"),
        name="attention",
    )(q, kv, kv, kv, bias, sink)


def _attn_bias_tables():
    rep = ATTN_HEADS // ATTN_KV_HEADS
    kj = np.arange(3 * WINDOW)[:, None]
    qi = np.arange(WINDOW)[None, :]
    dist = np.abs(kj - WINDOW - qi).astype(np.float32)
    in_window = dist <= WINDOW
    slopes = 2.0 ** (-8.0 * np.arange(1, ATTN_HEADS + 1, dtype=np.float32) / ATTN_HEADS)
    alibi = -slopes.reshape(ATTN_KV_HEADS, rep)[:, None, :, None] * dist[None, :, None, :]
    cases = []
    for has_prev, has_next in ((False, True), (True, True), (True, False)):
        ok = in_window & ((kj >= WINDOW) | has_prev) & ((kj < 2 * WINDOW) | has_next)
        cases.append(np.where(ok[None, :, None, :], alibi, np.float32(MASKED)))
    out = np.stack(cases).astype(np.float32)
    return out.reshape(3, ATTN_KV_HEADS, 3 * WINDOW, rep * WINDOW)


def _out_proj_body(x_ref, yf_ref, yb_ref, z_ref, ya_ref, nssm_ref, wo_ref, npost_ref, o_ref, *, d_ssm):
    y = yf_ref[...].astype(F32) + yb_ref[...].astype(F32)
    ys = _rmsnorm(y * _silu(z_ref[...].astype(F32)), nssm_ref[...]).astype(BF16)
    mix = (jnp.dot(ys, wo_ref[:d_ssm, :], preferred_element_type=F32)
           + jnp.dot(ya_ref[...], wo_ref[d_ssm:, :], preferred_element_type=F32))
    o_ref[...] = x_ref[...] + _rmsnorm(mix, npost_ref[...])


def _out_proj(x2, yf, yb, z, ya, nssm, wo, npost, *, tm):
    t, d = x2.shape
    d_ssm = yf.shape[1]
    row = lambda width: pl.BlockSpec((tm, width), lambda i: (i, 0))
    return pl.pallas_call(
        functools.partial(_out_proj_body, d_ssm=d_ssm),
        grid=(t // tm,),
        in_specs=[row(d), row(d_ssm), row(d_ssm), row(d_ssm), row(ya.shape[1]),
                  _resident((1, d_ssm)), _resident(wo.shape), _resident((1, d))],
        out_specs=row(d),
        out_shape=jax.ShapeDtypeStruct((t, d), F32),
        compiler_params=_params("parallel"),
        name="out_proj",
    )(x2, yf, yb, z, ya, nssm, wo, npost)


def _mlp_body(x_ref, npre_ref, wu_ref, wd_ref, npost_ref, o_ref):
    x = x_ref[...]
    h = _rmsnorm(x, npre_ref[...]).astype(BF16)
    u = jnp.maximum(jnp.dot(h, wu_ref[...], preferred_element_type=F32), 0.0)
    f = jnp.dot((u * u).astype(BF16), wd_ref[...], preferred_element_type=F32)
    o_ref[...] = x + _rmsnorm(f, npost_ref[...])


def _mlp(x2, npre, wu, wd, npost, *, tm):
    t, d = x2.shape
    row = pl.BlockSpec((tm, d), lambda i: (i, 0))
    return pl.pallas_call(
        _mlp_body,
        grid=(t // tm,),
        in_specs=[row, _resident((1, d)), _resident(wu.shape), _resident(wd.shape), _resident((1, d))],
        out_specs=row,
        out_shape=jax.ShapeDtypeStruct((t, d), F32),
        compiler_params=_params("parallel"),
        name="mlp",
    )(x2, npre, wu, wd, npost)


def _layer(x2, p, *, batch, seq):
    d = x2.shape[1]
    d_ssm = SSM_HEADS * SSM_HEAD_DIM
    d_xbc = d_ssm + 2 * SSM_GROUPS * SSM_STATE
    d_attn = ATTN_HEADS * ATTN_HEAD_DIM
    d_kv2 = 2 * ATTN_KV_HEADS * ATTN_HEAD_DIM
    n_dt = 2 * SSM_HEADS
    row = lambda v: v.reshape(1, -1).astype(F32)
    per_stream = lambda v: jnp.broadcast_to(v.reshape(n_dt, 1).astype(F32), (n_dt, CHUNK))

    w_in = p["w_in"]
    o_dt = d_ssm + d_xbc
    w_in = jnp.concatenate([w_in[:, :o_dt], w_in[:, o_dt + n_dt:], w_in[:, o_dt:o_dt + n_dt],
                            jnp.zeros((d, LANES - n_dt), w_in.dtype)], axis=1).astype(BF16)

    z, xc, bt, cc, acol, drow, q, kv = _in_proj(
        x2, row(p["norm_mix_pre"]), w_in, p["conv_w"].astype(F32), row(p["conv_b"]),
        per_stream(p["dt_bias"]), per_stream(p["a_log"]),
        seq=seq, tm=512, d_ssm=d_ssm, d_xbc=d_xbc, d_attn=d_attn, d_kv2=d_kv2)
    dskip = jnp.repeat(p["d_skip"].astype(F32), SSM_HEAD_DIM).reshape(1, d_ssm)
    yf, yb = _ssd_scan(xc, bt, cc, acol, drow, dskip, batch=batch, seq=seq)

    rep = ATTN_HEADS // ATTN_KV_HEADS
    sink = jnp.repeat(p["attn_sink"].astype(F32), WINDOW).reshape(ATTN_KV_HEADS, 1, rep * WINDOW)
    ya = _attention(q, kv, jnp.asarray(_attn_bias_tables()), sink, batch=batch, seq=seq)

    x2 = _out_proj(x2, yf, yb, z, ya, row(p["ssm_norm"]), p["w_out"].astype(BF16), row(p["norm_mix_post"]), tm=512)
    return _mlp(x2, row(p["norm_mlp_pre"]), p["w_up"].astype(BF16), p["w_down"].astype(BF16),
                row(p["norm_mlp_post"]), tm=512)


def kernel(x, norm_mix_pre, w_in, conv_w, conv_b, dt_bias, a_log, d_skip, ssm_norm, attn_sink, w_out,
           norm_mix_post, norm_mlp_pre, w_up, w_down, norm_mlp_post):
    batch, seq, d = x.shape
    names = ("norm_mix_pre", "w_in", "conv_w", "conv_b", "dt_bias", "a_log", "d_skip", "ssm_norm", "attn_sink",
             "w_out", "norm_mix_post", "norm_mlp_pre", "w_up", "w_down", "norm_mlp_post")
    stacked = (norm_mix_pre, w_in, conv_w, conv_b, dt_bias, a_log, d_skip, ssm_norm, attn_sink, w_out,
               norm_mix_post, norm_mlp_pre, w_up, w_down, norm_mlp_post)
    x2 = x.reshape(batch * seq, d)
    for i in range(w_in.shape[0]):
        x2 = _layer(x2, {k: v[i] for k, v in zip(names, stacked)}, batch=batch, seq=seq)
    return x2.reshape(batch, seq, d)
```

```python
import functools

import jax
import jax.numpy as jnp
import numpy as np
from jax import lax
from jax.experimental import pallas as pl
from jax.experimental.pallas import tpu as pltpu

F32 = jnp.float32
BF16 = jnp.bfloat16

EPS = 1e-6
SSM_HEADS = 16
SSM_HEAD_DIM = 64
SSM_GROUPS = 2
SSM_STATE = 128
CONV_WIDTH = 5
CHUNK = 128
ATTN_HEADS = 16
ATTN_KV_HEADS = 4
ATTN_HEAD_DIM = 64
WINDOW = 128
LANES = 128
SUBLANES = 8
MASKED = -1e30
LOG2E = 1.4426950408889634
VMEM_LIMIT = 56 * 1024 * 1024

NT_DIMS = (((1,), (1,)), ((), ()))
TN_DIMS = (((0,), (0,)), ((), ()))


def _rmsnorm(x, w):
    return x * lax.rsqrt(jnp.mean(x * x, axis=-1, keepdims=True) + EPS) * w


def _silu(x):
    return 0.5 * x * (1.0 + jnp.tanh(0.5 * x))


def _params(*semantics):
    return pltpu.CompilerParams(dimension_semantics=semantics, vmem_limit_bytes=VMEM_LIMIT)


def _resident(shape):
    return pl.BlockSpec(shape, lambda *_: (0,) * len(shape), pipeline_mode=pl.Buffered(1))


def _in_proj_body(x_ref, xprev_ref, xnext_ref, g_ref, w_ref, cw_ref, cb_ref, dtb_ref, alog_ref,
                  z_ref, xc_ref, bt_ref, cc_ref, acol_ref, drow_ref, q_ref, kv_ref,
                  ext_ref, *, tm, blocks_per_seq, d_ssm, d_xbc, d_attn, d_kv2):
    i = pl.program_id(0)
    halo = SUBLANES
    gs = SSM_GROUPS * SSM_STATE
    h = _rmsnorm(x_ref[...], g_ref[...]).astype(BF16)
    h_halo = _rmsnorm(jnp.concatenate([xprev_ref[...], xnext_ref[...]], axis=0), g_ref[...]).astype(BF16)
    he = jnp.concatenate([h, h_halo], axis=0)

    o_xbc, o_q, o_kv, o_dt = d_ssm, d_ssm + d_xbc, d_ssm + d_xbc + d_attn, d_ssm + d_xbc + d_attn + d_kv2
    first = i % blocks_per_seq == 0
    last = i % blocks_per_seq == blocks_per_seq - 1
    slab = 2 * LANES
    ext_rows = CHUNK + 2 * halo

    def xbc_slab(c0):
        xs = jnp.dot(he, w_ref[:, o_xbc + c0:o_xbc + c0 + slab], preferred_element_type=F32)
        ext_ref[0:halo, c0:c0 + slab] = jnp.where(first, 0.0, xs[tm:tm + halo])
        ext_ref[halo:halo + tm, c0:c0 + slab] = xs[0:tm]
        ext_ref[halo + tm:, c0:c0 + slab] = jnp.where(last, 0.0, xs[tm + halo:])

    def out_slab(o_ref, w0, c0, scale):
        acc = jnp.dot(h, w_ref[:, w0 + c0:w0 + c0 + slab], preferred_element_type=F32)
        o_ref[:, c0:c0 + slab] = (acc if scale == 1.0 else acc * scale).astype(BF16)

    def conv_strip(k, c0):
        r0 = k * CHUNK
        e = ext_ref[r0:r0 + ext_rows, c0:c0 + LANES]
        acc = cb_ref[:, c0:c0 + LANES] + cw_ref[2:3, c0:c0 + LANES] * e[halo:halo + CHUNK]
        for j in (0, 1, 3, 4):
            tap = pltpu.roll(e, (CONV_WIDTH // 2 - j) % ext_rows, 0)[halo:halo + CHUNK]
            acc = acc + cw_ref[j:j + 1, c0:c0 + LANES] * tap
        act = _silu(acc)
        if c0 < d_ssm:
            xc_ref[r0:r0 + CHUNK, c0:c0 + LANES] = act.astype(BF16)
        elif c0 < d_ssm + gs:
            g = (c0 - d_ssm) // SSM_STATE
            bt_ref[(k * SSM_GROUPS + g) * SSM_STATE:(k * SSM_GROUPS + g + 1) * SSM_STATE, :] = act.T.astype(BF16)
        else:
            cc_ref[r0:r0 + CHUNK, c0 - d_ssm - gs:c0 - d_ssm - gs + LANES] = act.astype(BF16)

    lane = lax.broadcasted_iota(jnp.int32, (2 * SSM_HEADS, CHUNK), 1)
    stream = lax.broadcasted_iota(jnp.int32, (2 * SSM_HEADS, CHUNK), 0)

    def decay_terms(dt_raw, k):
        r0 = k * CHUNK
        v = dt_raw[r0:r0 + CHUNK, :].T[0:2 * SSM_HEADS, :] + dtb_ref[...]
        dt = jnp.maximum(v, 0.0) + jnp.log1p(jnp.exp(-jnp.abs(v)))
        da = dt * (-jnp.exp(alog_ref[...]))
        pre = da
        sh = 1
        while sh < CHUNK:
            pre = pre + jnp.where(lane >= sh, pltpu.roll(pre, sh, 1), 0.0)
            sh *= 2
        total = jnp.broadcast_to(pre[:, CHUNK - 1:CHUNK], pre.shape)
        a = jnp.where(stream < SSM_HEADS, pre, total - pre + da)
        a2 = a * LOG2E
        grow = a2 - jnp.log2(dt)
        wrow = jnp.exp(total - a) * dt
        drow_ref[k * 4 * SSM_HEADS:(k + 1) * 4 * SSM_HEADS, :] = jnp.concatenate([grow, wrow], axis=0)
        a_pad = jnp.concatenate([a2, jnp.zeros((CHUNK - 2 * SSM_HEADS, CHUNK), F32)], axis=0)
        acol_ref[r0:r0 + CHUNK, :] = a_pad.T

    n_chunks = tm // CHUNK
    xbc_tasks = [functools.partial(xbc_slab, c0) for c0 in range(0, d_xbc, slab)]
    out_tasks = ([functools.partial(out_slab, z_ref, 0, c0, 1.0) for c0 in range(0, d_ssm, slab)]
                 + [functools.partial(out_slab, q_ref, o_q, c0, ATTN_HEAD_DIM ** -0.5) for c0 in range(0, d_attn, slab)]
                 + [functools.partial(out_slab, kv_ref, o_kv, c0, 1.0) for c0 in range(0, d_kv2, slab)])
    mxu_tasks = [xbc_tasks[0]]
    for n_x in range(1, len(xbc_tasks)):
        mxu_tasks += [xbc_tasks[n_x]] + ([out_tasks.pop(0)] if out_tasks else [])
    mxu_tasks += out_tasks
    strips = [(k, c0) for c0 in range(0, d_xbc, LANES) for k in range(n_chunks)]
    per_task = -(-len(strips) // (len(mxu_tasks) - 1))
    dt_raw = jnp.dot(h, w_ref[:, o_dt:o_dt + LANES], preferred_element_type=F32)
    decay_chunks = list(range(n_chunks))
    slabs_done = 0
    for task in mxu_tasks:
        task()
        slabs_done += task.func is xbc_slab
        if decay_chunks:
            decay_terms(dt_raw, decay_chunks.pop(0))
        for _ in range(per_task):
            if strips and strips[0][1] < slabs_done * slab:
                conv_strip(*strips.pop(0))
    for strip in strips:
        conv_strip(*strip)


def _in_proj(x2, gain, w, conv_w, conv_b, dt_bias, a_log, *, seq, tm, d_ssm, d_xbc, d_attn, d_kv2):
    t, d = x2.shape
    n = w.shape[1]
    halo = SUBLANES
    nsub = tm // halo
    nhalo = t // halo
    chunks = tm // CHUNK
    gs = SSM_GROUPS * SSM_STATE
    row = lambda rows, width: pl.BlockSpec((rows, width), lambda i: (i, 0))
    out_shape = [
        jax.ShapeDtypeStruct((t, d_ssm), BF16),
        jax.ShapeDtypeStruct((t, d_ssm), BF16),
        jax.ShapeDtypeStruct((t // CHUNK * gs, CHUNK), BF16),
        jax.ShapeDtypeStruct((t, gs), BF16),
        jax.ShapeDtypeStruct((t, LANES), F32),
        jax.ShapeDtypeStruct((t // CHUNK * 4 * SSM_HEADS, CHUNK), F32),
        jax.ShapeDtypeStruct((t, d_attn), BF16),
        jax.ShapeDtypeStruct((t, d_kv2), BF16),
    ]
    return pl.pallas_call(
        functools.partial(_in_proj_body, tm=tm, blocks_per_seq=seq // tm, d_ssm=d_ssm, d_xbc=d_xbc,
                          d_attn=d_attn, d_kv2=d_kv2),
        grid=(t // tm,),
        in_specs=[row(tm, d),
                  pl.BlockSpec((halo, d), lambda i: (jnp.maximum(i * nsub - 1, 0), 0)),
                  pl.BlockSpec((halo, d), lambda i: (jnp.minimum((i + 1) * nsub, nhalo - 1), 0)),
                  _resident((1, d)), _resident((d, n)), _resident((CONV_WIDTH, d_xbc)), _resident((1, d_xbc)),
                  _resident((2 * SSM_HEADS, CHUNK)), _resident((2 * SSM_HEADS, CHUNK))],
        out_specs=[row(tm, d_ssm), row(tm, d_ssm), row(chunks * gs, CHUNK), row(tm, gs), row(tm, LANES),
                   row(chunks * 4 * SSM_HEADS, CHUNK), row(tm, d_attn), row(tm, d_kv2)],
        out_shape=out_shape,
        scratch_shapes=[pltpu.VMEM((tm + 2 * halo, d_xbc), F32)],
        compiler_params=_params("parallel"),
        name="in_proj",
    )(x2, x2, x2, gain, w, conv_w, conv_b, dt_bias, a_log)


def _ssd_scan_body(xf_ref, btf_ref, ccf_ref, acf_ref, drf_ref, xb_ref, btb_ref, ccb_ref, acb_ref, drb_ref,
                   dskip_ref, yf_ref, yb_ref, stf_ref, stb_ref):
    @pl.when(pl.program_id(1) == 0)
    def _():
        stf_ref[...] = jnp.zeros_like(stf_ref)
        stb_ref[...] = jnp.zeros_like(stb_ref)

    gw = (SSM_HEADS // SSM_GROUPS) * SSM_HEAD_DIM
    pairs_per_group = gw // LANES
    row = lax.broadcasted_iota(jnp.int32, (CHUNK, CHUNK), 0)
    lane = lax.broadcasted_iota(jnp.int32, (CHUNK, CHUNK), 1)
    first_head = lane < SSM_HEAD_DIM

    def visit(x_ref, bt_ref, cc_ref, ac, dr, st_ref, y_ref, direction):
        off = direction * SSM_HEADS
        inside = row >= lane if direction == 0 else row <= lane
        edge = CHUNK - 1 if direction == 0 else 0
        for g in range(SSM_GROUPS):
            btg = bt_ref[g * SSM_STATE:(g + 1) * SSM_STATE, :]
            cg = cc_ref[:, g * SSM_STATE:(g + 1) * SSM_STATE]
            cb = jnp.dot(cg, btg, preferred_element_type=F32)
            st = st_ref[:, g * gw:(g + 1) * gw]
            y_off = jnp.dot(cg, st.astype(BF16), preferred_element_type=F32)
            for jj in range(pairs_per_group):
                j = g * pairs_per_group + jj
                sl = slice(j * LANES, (j + 1) * LANES)
                decays, weighted_bt, spans = [], [], []
                for h in (2 * j, 2 * j + 1):
                    col = jnp.broadcast_to(ac[:, off + h:off + h + 1], (CHUNK, CHUNK))
                    grow = dr[off + h:off + h + 1, :]
                    wrow = dr[2 * SSM_HEADS + off + h:2 * SSM_HEADS + off + h + 1, :]
                    decays.append((jnp.where(inside, jnp.exp2(col - grow), 0.0) * cb).astype(BF16))
                    weighted_bt.append(btg * wrow.astype(BF16))
                    spans.append(jnp.exp2(col))
                lhs = jnp.concatenate([jnp.concatenate(decays, axis=1), jnp.concatenate(weighted_bt, axis=1)], axis=0)
                xp = x_ref[:, sl]
                zero = jnp.zeros_like(xp)
                rhs = jnp.concatenate([jnp.where(first_head, xp, zero), jnp.where(first_head, zero, xp)], axis=0)
                out = jnp.dot(lhs, rhs, preferred_element_type=F32)
                ea = jnp.where(first_head, spans[0], spans[1])
                y = out[0:CHUNK] + ea * y_off[:, jj * LANES:(jj + 1) * LANES]
                if direction == 0:
                    y = y + xp.astype(F32) * dskip_ref[:, sl]
                y_ref[:, sl] = y.astype(BF16)
                st_ref[:, sl] = st[:, jj * LANES:(jj + 1) * LANES] * ea[edge:edge + 1, :] + out[CHUNK:]

    visit(xf_ref, btf_ref, ccf_ref, acf_ref[...], drf_ref[...], stf_ref, yf_ref, 0)
    visit(xb_ref, btb_ref, ccb_ref, acb_ref[...], drb_ref[...], stb_ref, yb_ref, 1)


def _ssd_scan(xc, bt, cc, acol, drow, dskip, *, batch, seq):
    t, d_ssm = xc.shape
    nc = seq // CHUNK
    gs = SSM_GROUPS * SSM_STATE
    fwd = lambda rows, width: pl.BlockSpec((rows, width), lambda b, c: (b * nc + c, 0))
    bwd = lambda rows, width: pl.BlockSpec((rows, width), lambda b, c: (b * nc + nc - 1 - c, 0))
    operands = lambda spec: [spec(CHUNK, d_ssm), spec(gs, CHUNK), spec(CHUNK, gs), spec(CHUNK, LANES),
                             spec(4 * SSM_HEADS, CHUNK)]
    return pl.pallas_call(
        _ssd_scan_body,
        grid=(batch, nc),
        in_specs=operands(fwd) + operands(bwd) + [_resident((1, d_ssm))],
        out_specs=[fwd(CHUNK, d_ssm), bwd(CHUNK, d_ssm)],
        out_shape=[jax.ShapeDtypeStruct((t, d_ssm), BF16), jax.ShapeDtypeStruct((t, d_ssm), BF16)],
        scratch_shapes=[pltpu.VMEM((SSM_STATE, d_ssm), F32), pltpu.VMEM((SSM_STATE, d_ssm), F32)],
        compiler_params=_params("arbitrary", "arbitrary"),
        name="ssd_scan",
    )(xc, bt, cc, acol, drow, xc, bt, cc, acol, drow, dskip)


def _attn_body(q_ref, kvp_ref, kvc_ref, kvn_ref, bias_ref, sink_ref, o_ref, s_ref):
    rep = ATTN_HEADS // ATTN_KV_HEADS
    d_kv = ATTN_KV_HEADS * ATTN_HEAD_DIM
    kv = jnp.concatenate([kvp_ref[...], kvc_ref[...], kvn_ref[...]], axis=0)
    row_max = []
    for g in range(ATTN_KV_HEADS):
        kg = kv[:, g * ATTN_HEAD_DIM:(g + 1) * ATTN_HEAD_DIM]
        qg = jnp.concatenate([q_ref[:, (g * rep + r) * ATTN_HEAD_DIM:(g * rep + r + 1) * ATTN_HEAD_DIM]
                              for r in range(rep)], axis=0)
        s = lax.dot_general(kg, qg, NT_DIMS, preferred_element_type=F32) + bias_ref[0, g]
        s_ref[g] = s
        row_max.append(jnp.maximum(jnp.max(s, axis=0, keepdims=True), sink_ref[g]))
    for g in range(ATTN_KV_HEADS):
        vg = kv[:, d_kv + g * ATTN_HEAD_DIM:d_kv + (g + 1) * ATTN_HEAD_DIM]
        m = row_max[g]
        p = jnp.exp(s_ref[g] - m)
        denom = jnp.sum(p, axis=0, keepdims=True) + jnp.exp(sink_ref[g] - m)
        o = lax.dot_general(vg, p.astype(BF16), TN_DIMS, preferred_element_type=F32) / denom
        for r in range(0, rep, 2):
            two = jnp.concatenate([o[:, r * WINDOW:(r + 1) * WINDOW], o[:, (r + 1) * WINDOW:(r + 2) * WINDOW]], axis=0)
            lo = (g * rep + r) * ATTN_HEAD_DIM
            o_ref[:, lo:lo + 2 * ATTN_HEAD_DIM] = two.T.astype(BF16)


def _attention(q, kv, bias, sink, *, batch, seq):
    t, d_attn = q.shape
    nb = seq // WINDOW
    rep = ATTN_HEADS // ATTN_KV_HEADS
    blk = lambda f: pl.BlockSpec((WINDOW, kv.shape[1]), f)
    return pl.pallas_call(
        _attn_body,
        grid=(batch, nb),
        in_specs=[
            pl.BlockSpec((WINDOW, d_attn), lambda b, n: (b * nb + n, 0)),
            blk(lambda b, n: (b * nb + jnp.maximum(n - 1, 0), 0)),
            blk(lambda b, n: (b * nb + n, 0)),
            blk(lambda b, n: (b * nb + jnp.minimum(n + 1, nb - 1), 0)),
            pl.BlockSpec((1, ATTN_KV_HEADS, 3 * WINDOW, rep * WINDOW),
                         lambda b, n: (jnp.where(n == 0, 0, jnp.where(n == nb - 1, 2, 1)), 0, 0, 0)),
            _resident((ATTN_KV_HEADS, 1, rep * WINDOW)),
        ],
        out_specs=pl.BlockSpec((WINDOW, d_attn), lambda b, n: (b * nb + n, 0)),
        out_shape=jax.ShapeDtypeStruct((t, d_attn), BF16),
        scratch_shapes=[pltpu.VMEM((ATTN_KV_HEADS, 3 * WINDOW, rep * WINDOW), F32)],
        compiler_params=_params("parallel", "parallel"),
        name="attention",
    )(q, kv, kv, kv, bias, sink)


def _attn_bias_tables():
    rep = ATTN_HEADS // ATTN_KV_HEADS
    kj = np.arange(3 * WINDOW)[:, None]
    qi = np.arange(WINDOW)[None, :]
    dist = np.abs(kj - WINDOW - qi).astype(np.float32)
    in_window = dist <= WINDOW
    slopes = 2.0 ** (-8.0 * np.arange(1, ATTN_HEADS + 1, dtype=np.float32) / ATTN_HEADS)
    alibi = -slopes.reshape(ATTN_KV_HEADS, rep)[:, None, :, None] * dist[None, :, None, :]
    cases = []
    for has_prev, has_next in ((False, True), (True, True), (True, False)):
        ok = in_window & ((kj >= WINDOW) | has_prev) & ((kj < 2 * WINDOW) | has_next)
        cases.append(np.where(ok[None, :, None, :], alibi, np.float32(MASKED)))
    out = np.stack(cases).astype(np.float32)
    return out.reshape(3, ATTN_KV_HEADS, 3 * WINDOW, rep * WINDOW)


def _out_proj_body(x_ref, yf_ref, yb_ref, z_ref, ya_ref, nssm_ref, wo_ref, npost_ref, o_ref, *, d_ssm):
    y = yf_ref[...].astype(F32) + yb_ref[...].astype(F32)
    ys = _rmsnorm(y * _silu(z_ref[...].astype(F32)), nssm_ref[...]).astype(BF16)
    mix = (jnp.dot(ys, wo_ref[:d_ssm, :], preferred_element_type=F32)
           + jnp.dot(ya_ref[...], wo_ref[d_ssm:, :], preferred_element_type=F32))
    o_ref[...] = x_ref[...] + _rmsnorm(mix, npost_ref[...])


def _out_proj(x2, yf, yb, z, ya, nssm, wo, npost, *, tm):
    t, d = x2.shape
    d_ssm = yf.shape[1]
    row = lambda width: pl.BlockSpec((tm, width), lambda i: (i, 0))
    return pl.pallas_call(
        functools.partial(_out_proj_body, d_ssm=d_ssm),
        grid=(t // tm,),
        in_specs=[row(d), row(d_ssm), row(d_ssm), row(d_ssm), row(ya.shape[1]),
                  _resident((1, d_ssm)), _resident(wo.shape), _resident((1, d))],
        out_specs=row(d),
        out_shape=jax.ShapeDtypeStruct((t, d), F32),
        compiler_params=_params("parallel"),
        name="out_proj",
    )(x2, yf, yb, z, ya, nssm, wo, npost)


def _mlp_body(x_ref, npre_ref, wu_ref, wd_ref, npost_ref, o_ref):
    x = x_ref[...]
    h = _rmsnorm(x, npre_ref[...]).astype(BF16)
    u = jnp.maximum(jnp.dot(h, wu_ref[...], preferred_element_type=F32), 0.0)
    f = jnp.dot((u * u).astype(BF16), wd_ref[...], preferred_element_type=F32)
    o_ref[...] = x + _rmsnorm(f, npost_ref[...])


def _mlp(x2, npre, wu, wd, npost, *, tm):
    t, d = x2.shape
    row = pl.BlockSpec((tm, d), lambda i: (i, 0))
    return pl.pallas_call(
        _mlp_body,
        grid=(t // tm,),
        in_specs=[row, _resident((1, d)), _resident(wu.shape), _resident(wd.shape), _resident((1, d))],
        out_specs=row,
        out_shape=jax.ShapeDtypeStruct((t, d), F32),
        compiler_params=_params("parallel"),
        name="mlp",
    )(x2, npre, wu, wd, npost)


def _layer(x2, p, *, batch, seq):
    d = x2.shape[1]
    d_ssm = SSM_HEADS * SSM_HEAD_DIM
    d_xbc = d_ssm + 2 * SSM_GROUPS * SSM_STATE
    d_attn = ATTN_HEADS * ATTN_HEAD_DIM
    d_kv2 = 2 * ATTN_KV_HEADS * ATTN_HEAD_DIM
    n_dt = 2 * SSM_HEADS
    row = lambda v: v.reshape(1, -1).astype(F32)
    per_stream = lambda v: jnp.broadcast_to(v.reshape(n_dt, 1).astype(F32), (n_dt, CHUNK))

    w_in = p["w_in"]
    o_dt = d_ssm + d_xbc
    w_in = jnp.concatenate([w_in[:, :o_dt], w_in[:, o_dt + n_dt:], w_in[:, o_dt:o_dt + n_dt],
                            jnp.zeros((d, LANES - n_dt), w_in.dtype)], axis=1).astype(BF16)

    z, xc, bt, cc, acol, drow, q, kv = _in_proj(
        x2, row(p["norm_mix_pre"]), w_in, p["conv_w"].astype(F32), row(p["conv_b"]),
        per_stream(p["dt_bias"]), per_stream(p["a_log"]),
        seq=seq, tm=512, d_ssm=d_ssm, d_xbc=d_xbc, d_attn=d_attn, d_kv2=d_kv2)
    dskip = jnp.repeat(p["d_skip"].astype(F32), SSM_HEAD_DIM).reshape(1, d_ssm)
    yf, yb = _ssd_scan(xc, bt, cc, acol, drow, dskip, batch=batch, seq=seq)

    rep = ATTN_HEADS // ATTN_KV_HEADS
    sink = jnp.repeat(p["attn_sink"].astype(F32), WINDOW).reshape(ATTN_KV_HEADS, 1, rep * WINDOW)
    ya = _attention(q, kv, jnp.asarray(_attn_bias_tables()), sink, batch=batch, seq=seq)

    x2 = _out_proj(x2, yf, yb, z, ya, row(p["ssm_norm"]), p["w_out"].astype(BF16), row(p["norm_mix_post"]), tm=512)
    return _mlp(x2, row(p["norm_mlp_pre"]), p["w_up"].astype(BF16), p["w_down"].astype(BF16),
                row(p["norm_mlp_post"]), tm=512)


def kernel(x, norm_mix_pre, w_in, conv_w, conv_b, dt_bias, a_log, d_skip, ssm_norm, attn_sink, w_out,
           norm_mix_post, norm_mlp_pre, w_up, w_down, norm_mlp_post):
    batch, seq, d = x.shape
    names = ("norm_mix_pre", "w_in", "conv_w", "conv_b", "dt_bias", "a_log", "d_skip", "ssm_norm", "attn_sink",
             "w_out", "norm_mix_post", "norm_mlp_pre", "w_up", "w_down", "norm_mlp_post")
    stacked = (norm_mix_pre, w_in, conv_w, conv_b, dt_bias, a_log, d_skip, ssm_norm, attn_sink, w_out,
               norm_mix_post, norm_mlp_pre, w_up, w_down, norm_mlp_post)
    x2 = x.reshape(batch * seq, d)
    for i in range(w_in.shape[0]):
        x2 = _layer(x2, {k: v[i] for k, v in zip(names, stacked)}, batch=batch, seq=seq)
    return x2.reshape(batch, seq, d)
```

```python
import functools

import jax
import jax.numpy as jnp
import numpy as np
from jax import lax
from jax.experimental import pallas as pl
from jax.experimental.pallas import tpu as pltpu

F32 = jnp.float32
BF16 = jnp.bfloat16

EPS = 1e-6
SSM_HEADS = 16
SSM_HEAD_DIM = 64
SSM_GROUPS = 2
SSM_STATE = 128
CONV_WIDTH = 5
CHUNK = 128
ATTN_HEADS = 16
ATTN_KV_HEADS = 4
ATTN_HEAD_DIM = 64
WINDOW = 128
LANES = 128
SUBLANES = 8
MASKED = -1e30
LOG2E = 1.4426950408889634
BF16_HUGE = 3e38
VMEM_LIMIT = 56 * 1024 * 1024

NT_DIMS = (((1,), (1,)), ((), ()))
TN_DIMS = (((0,), (0,)), ((), ()))


def _rmsnorm(x, w):
    return x * lax.rsqrt(jnp.mean(x * x, axis=-1, keepdims=True) + EPS) * w


def _silu(x):
    return 0.5 * x * (1.0 + jnp.tanh(0.5 * x))


def _params(*semantics):
    return pltpu.CompilerParams(dimension_semantics=semantics, vmem_limit_bytes=VMEM_LIMIT)


def _resident(shape):
    return pl.BlockSpec(shape, lambda *_: (0,) * len(shape), pipeline_mode=pl.Buffered(1))


def _in_proj_body(x_ref, xprev_ref, xnext_ref, g_ref, w_ref, cw_ref, cb_ref, dtb_ref, alog_ref,
                  z_ref, xc_ref, bt_ref, cc_ref, acol_ref, drow_ref, q_ref, kv_ref,
                  ext_ref, *, tm, blocks_per_seq, d_ssm, d_xbc, d_attn, d_kv2):
    i = pl.program_id(0)
    halo = SUBLANES
    gs = SSM_GROUPS * SSM_STATE
    h = _rmsnorm(x_ref[...], g_ref[...]).astype(BF16)
    h_halo = _rmsnorm(jnp.concatenate([xprev_ref[...], xnext_ref[...]], axis=0), g_ref[...]).astype(BF16)
    he = jnp.concatenate([h, h_halo], axis=0)

    o_xbc, o_q, o_kv, o_dt = d_ssm, d_ssm + d_xbc, d_ssm + d_xbc + d_attn, d_ssm + d_xbc + d_attn + d_kv2
    first = i % blocks_per_seq == 0
    last = i % blocks_per_seq == blocks_per_seq - 1
    slab = 2 * LANES
    ext_rows = CHUNK + 2 * halo

    def xbc_slab(c0):
        xs = jnp.dot(he, w_ref[:, o_xbc + c0:o_xbc + c0 + slab], preferred_element_type=F32)
        ext_ref[0:halo, c0:c0 + slab] = jnp.where(first, 0.0, xs[tm:tm + halo])
        ext_ref[halo:halo + tm, c0:c0 + slab] = xs[0:tm]
        ext_ref[halo + tm:, c0:c0 + slab] = jnp.where(last, 0.0, xs[tm + halo:])

    def out_slab(o_ref, w0, c0, scale):
        acc = jnp.dot(h, w_ref[:, w0 + c0:w0 + c0 + slab], preferred_element_type=F32)
        o_ref[:, c0:c0 + slab] = (acc if scale == 1.0 else acc * scale).astype(BF16)

    def conv_strip(k, c0):
        r0 = k * CHUNK
        e = ext_ref[r0:r0 + ext_rows, c0:c0 + LANES]
        acc = cb_ref[:, c0:c0 + LANES] + cw_ref[2:3, c0:c0 + LANES] * e[halo:halo + CHUNK]
        for j in (0, 1, 3, 4):
            tap = pltpu.roll(e, (CONV_WIDTH // 2 - j) % ext_rows, 0)[halo:halo + CHUNK]
            acc = acc + cw_ref[j:j + 1, c0:c0 + LANES] * tap
        act = _silu(acc)
        if c0 < d_ssm:
            xc_ref[r0:r0 + CHUNK, c0:c0 + LANES] = act.astype(BF16)
        elif c0 < d_ssm + gs:
            g = (c0 - d_ssm) // SSM_STATE
            bt_ref[(k * SSM_GROUPS + g) * SSM_STATE:(k * SSM_GROUPS + g + 1) * SSM_STATE, :] = act.T.astype(BF16)
        else:
            cc_ref[r0:r0 + CHUNK, c0 - d_ssm - gs:c0 - d_ssm - gs + LANES] = act.astype(BF16)

    lane = lax.broadcasted_iota(jnp.int32, (2 * SSM_HEADS, CHUNK), 1)
    stream = lax.broadcasted_iota(jnp.int32, (2 * SSM_HEADS, CHUNK), 0)

    def decay_terms(dt_raw, k):
        r0 = k * CHUNK
        v = dt_raw[r0:r0 + CHUNK, :].T[0:2 * SSM_HEADS, :] + dtb_ref[...]
        dt = jnp.maximum(v, 0.0) + jnp.log1p(jnp.exp(-jnp.abs(v)))
        da = dt * (-jnp.exp(alog_ref[...]))
        pre = da
        sh = 1
        while sh < CHUNK:
            pre = pre + jnp.where(lane >= sh, pltpu.roll(pre, sh, 1), 0.0)
            sh *= 2
        total = jnp.broadcast_to(pre[:, CHUNK - 1:CHUNK], pre.shape)
        a = jnp.where(stream < SSM_HEADS, pre, total - pre + da)
        a2 = a * LOG2E
        grow = a2 - jnp.log2(dt)
        wrow = jnp.exp(total - a) * dt
        drow_ref[k * 4 * SSM_HEADS:(k + 1) * 4 * SSM_HEADS, :] = jnp.concatenate([grow, wrow], axis=0)
        a_pad = jnp.concatenate([a2, jnp.zeros((CHUNK - 2 * SSM_HEADS, CHUNK), F32)], axis=0)
        acol_ref[r0:r0 + CHUNK, :] = a_pad.T

    n_chunks = tm // CHUNK
    xbc_tasks = [functools.partial(xbc_slab, c0) for c0 in range(0, d_xbc, slab)]
    out_tasks = ([functools.partial(out_slab, z_ref, 0, c0, 1.0) for c0 in range(0, d_ssm, slab)]
                 + [functools.partial(out_slab, q_ref, o_q, c0, ATTN_HEAD_DIM ** -0.5) for c0 in range(0, d_attn, slab)]
                 + [functools.partial(out_slab, kv_ref, o_kv, c0, 1.0) for c0 in range(0, d_kv2, slab)])
    mxu_tasks = [xbc_tasks[0]]
    for n_x in range(1, len(xbc_tasks)):
        mxu_tasks += [xbc_tasks[n_x]] + ([out_tasks.pop(0)] if out_tasks else [])
    mxu_tasks += out_tasks
    strips = [(k, c0) for c0 in range(0, d_xbc, LANES) for k in range(n_chunks)]
    per_task = -(-len(strips) // (len(mxu_tasks) - 1))
    dt_raw = jnp.dot(h, w_ref[:, o_dt:o_dt + LANES], preferred_element_type=F32)
    decay_chunks = list(range(n_chunks))
    ready_cols = 0
    for task in mxu_tasks:
        task()
        if decay_chunks:
            decay_terms(dt_raw, decay_chunks.pop(0))
        for _ in range(per_task):
            if strips and strips[0][1] < ready_cols:
                conv_strip(*strips.pop(0))
        ready_cols += slab * (task.func is xbc_slab)
    for strip in strips:
        conv_strip(*strip)


def _in_proj(x2, gain, w, conv_w, conv_b, dt_bias, a_log, *, seq, tm, d_ssm, d_xbc, d_attn, d_kv2):
    t, d = x2.shape
    n = w.shape[1]
    halo = SUBLANES
    nsub = tm // halo
    nhalo = t // halo
    chunks = tm // CHUNK
    gs = SSM_GROUPS * SSM_STATE
    row = lambda rows, width: pl.BlockSpec((rows, width), lambda i: (i, 0))
    out_shape = [
        jax.ShapeDtypeStruct((t, d_ssm), BF16),
        jax.ShapeDtypeStruct((t, d_ssm), BF16),
        jax.ShapeDtypeStruct((t // CHUNK * gs, CHUNK), BF16),
        jax.ShapeDtypeStruct((t, gs), BF16),
        jax.ShapeDtypeStruct((t, LANES), F32),
        jax.ShapeDtypeStruct((t // CHUNK * 4 * SSM_HEADS, CHUNK), F32),
        jax.ShapeDtypeStruct((t, d_attn), BF16),
        jax.ShapeDtypeStruct((t, d_kv2), BF16),
    ]
    return pl.pallas_call(
        functools.partial(_in_proj_body, tm=tm, blocks_per_seq=seq // tm, d_ssm=d_ssm, d_xbc=d_xbc,
                          d_attn=d_attn, d_kv2=d_kv2),
        grid=(t // tm,),
        in_specs=[row(tm, d),
                  pl.BlockSpec((halo, d), lambda i: (jnp.maximum(i * nsub - 1, 0), 0)),
                  pl.BlockSpec((halo, d), lambda i: (jnp.minimum((i + 1) * nsub, nhalo - 1), 0)),
                  _resident((1, d)), _resident((d, n)), _resident((CONV_WIDTH, d_xbc)), _resident((1, d_xbc)),
                  _resident((2 * SSM_HEADS, CHUNK)), _resident((2 * SSM_HEADS, CHUNK))],
        out_specs=[row(tm, d_ssm), row(tm, d_ssm), row(chunks * gs, CHUNK), row(tm, gs), row(tm, LANES),
                   row(chunks * 4 * SSM_HEADS, CHUNK), row(tm, d_attn), row(tm, d_kv2)],
        out_shape=out_shape,
        scratch_shapes=[pltpu.VMEM((tm + 2 * halo, d_xbc), F32)],
        compiler_params=_params("parallel"),
        name="in_proj",
    )(x2, x2, x2, gain, w, conv_w, conv_b, dt_bias, a_log)


def _ssd_scan_body(xf_ref, btf_ref, ccf_ref, acf_ref, drf_ref, xb_ref, btb_ref, ccb_ref, acb_ref, drb_ref,
                   dskip_ref, yf_ref, yb_ref, stf_ref, stb_ref):
    @pl.when(pl.program_id(1) == 0)
    def _():
        stf_ref[...] = jnp.zeros_like(stf_ref)
        stb_ref[...] = jnp.zeros_like(stb_ref)

    gw = (SSM_HEADS // SSM_GROUPS) * SSM_HEAD_DIM
    pairs_per_group = gw // LANES
    row = lax.broadcasted_iota(jnp.int32, (CHUNK, CHUNK), 0)
    lane = lax.broadcasted_iota(jnp.int32, (CHUNK, CHUNK), 1)
    first_head = lane < SSM_HEAD_DIM
    keep_first = jnp.where(first_head, 1.0, 0.0).astype(BF16)
    keep_second = jnp.where(first_head, 0.0, 1.0).astype(BF16)

    def visit(x_ref, bt_ref, cc_ref, ac, dr, st_ref, y_ref, direction):
        off = direction * SSM_HEADS
        inside = row >= lane if direction == 0 else row <= lane
        edge = CHUNK - 1 if direction == 0 else 0
        for g in range(SSM_GROUPS):
            btg = bt_ref[g * SSM_STATE:(g + 1) * SSM_STATE, :]
            cg = cc_ref[:, g * SSM_STATE:(g + 1) * SSM_STATE]
            cb = jnp.dot(cg, btg, preferred_element_type=F32)
            cbm = jnp.where(inside, cb, 0.0).astype(BF16)
            btf = btg.astype(F32)
            st = st_ref[:, g * gw:(g + 1) * gw]
            y_off = jnp.dot(cg, st.astype(BF16), preferred_element_type=F32)
            for jj in range(pairs_per_group):
                j = g * pairs_per_group + jj
                sl = slice(j * LANES, (j + 1) * LANES)
                decays, weighted_bt, cols = [], [], []
                for h in (2 * j, 2 * j + 1):
                    col = jnp.broadcast_to(ac[:, off + h:off + h + 1], (CHUNK, CHUNK))
                    grow = dr[off + h:off + h + 1, :]
                    wrow = dr[2 * SSM_HEADS + off + h:2 * SSM_HEADS + off + h + 1, :]
                    decay = jnp.minimum(jnp.exp2(col - grow).astype(BF16), BF16_HUGE)
                    decays.append(decay * cbm)
                    weighted_bt.append((btf * wrow).astype(BF16))
                    cols.append(col)
                lhs = jnp.concatenate([jnp.concatenate(decays, axis=1), jnp.concatenate(weighted_bt, axis=1)], axis=0)
                xp = x_ref[:, sl]
                rhs = jnp.concatenate([xp * keep_first, xp * keep_second], axis=0)
                out = jnp.dot(lhs, rhs, preferred_element_type=F32)
                ea = jnp.exp2(jnp.where(first_head, cols[0], cols[1]))
                y = out[0:CHUNK] + ea * y_off[:, jj * LANES:(jj + 1) * LANES]
                if direction == 0:
                    y = y + xp.astype(F32) * dskip_ref[:, sl]
                y_ref[:, sl] = y.astype(BF16)
                st_ref[:, sl] = st[:, jj * LANES:(jj + 1) * LANES] * ea[edge:edge + 1, :] + out[CHUNK:]

    visit(xf_ref, btf_ref, ccf_ref, acf_ref[...], drf_ref[...], stf_ref, yf_ref, 0)
    visit(xb_ref, btb_ref, ccb_ref, acb_ref[...], drb_ref[...], stb_ref, yb_ref, 1)


def _ssd_scan(xc, bt, cc, acol, drow, dskip, *, batch, seq):
    t, d_ssm = xc.shape
    nc = seq // CHUNK
    gs = SSM_GROUPS * SSM_STATE
    fwd = lambda rows, width: pl.BlockSpec((rows, width), lambda b, c: (b * nc + c, 0))
    bwd = lambda rows, width: pl.BlockSpec((rows, width), lambda b, c: (b * nc + nc - 1 - c, 0))
    operands = lambda spec: [spec(CHUNK, d_ssm), spec(gs, CHUNK), spec(CHUNK, gs), spec(CHUNK, LANES),
                             spec(4 * SSM_HEADS, CHUNK)]
    return pl.pallas_call(
        _ssd_scan_body,
        grid=(batch, nc),
        in_specs=operands(fwd) + operands(bwd) + [_resident((1, d_ssm))],
        out_specs=[fwd(CHUNK, d_ssm), bwd(CHUNK, d_ssm)],
        out_shape=[jax.ShapeDtypeStruct((t, d_ssm), BF16), jax.ShapeDtypeStruct((t, d_ssm), BF16)],
        scratch_shapes=[pltpu.VMEM((SSM_STATE, d_ssm), F32), pltpu.VMEM((SSM_STATE, d_ssm), F32)],
        compiler_params=_params("arbitrary", "arbitrary"),
        name="ssd_scan",
    )(xc, bt, cc, acol, drow, xc, bt, cc, acol, drow, dskip)


ATTN_BLOCKS_PER_STEP = 2


def _attn_body(q_ref, kvp_ref, kvc_ref, kvn_ref, bias_lo_ref, bias_hi_ref, sink_ref, o_ref, s_ref):
    rep = ATTN_HEADS // ATTN_KV_HEADS
    d_kv = ATTN_KV_HEADS * ATTN_HEAD_DIM
    kv = jnp.concatenate([kvp_ref[...], kvc_ref[...], kvn_ref[...]], axis=0)
    bias_refs = (bias_lo_ref, bias_hi_ref)
    units = [(blk, g) for blk in range(ATTN_BLOCKS_PER_STEP) for g in range(ATTN_KV_HEADS)]
    row_max = {}
    for blk, g in units:
        rows = slice(blk * WINDOW, (blk + 1) * WINDOW)
        kg = kv[blk * WINDOW:(blk + 3) * WINDOW, g * ATTN_HEAD_DIM:(g + 1) * ATTN_HEAD_DIM]
        qg = jnp.concatenate([q_ref[rows, (g * rep + r) * ATTN_HEAD_DIM:(g * rep + r + 1) * ATTN_HEAD_DIM]
                              for r in range(rep)], axis=0)
        s = lax.dot_general(kg, qg, NT_DIMS, preferred_element_type=F32) + bias_refs[blk][0, g]
        s_ref[blk, g] = s
        row_max[blk, g] = jnp.maximum(jnp.max(s, axis=0, keepdims=True), sink_ref[g])
    for blk, g in units:
        rows = slice(blk * WINDOW, (blk + 1) * WINDOW)
        vg = kv[blk * WINDOW:(blk + 3) * WINDOW, d_kv + g * ATTN_HEAD_DIM:d_kv + (g + 1) * ATTN_HEAD_DIM]
        m = row_max[blk, g]
        p = jnp.exp(s_ref[blk, g] - m)
        denom = jnp.sum(p, axis=0, keepdims=True) + jnp.exp(sink_ref[g] - m)
        o = lax.dot_general(vg, p.astype(BF16), TN_DIMS, preferred_element_type=F32) / denom
        for r in range(0, rep, 2):
            two = jnp.concatenate([o[:, r * WINDOW:(r + 1) * WINDOW], o[:, (r + 1) * WINDOW:(r + 2) * WINDOW]], axis=0)
            lo = (g * rep + r) * ATTN_HEAD_DIM
            o_ref[rows, lo:lo + 2 * ATTN_HEAD_DIM] = two.T.astype(BF16)


def _attention(q, kv, bias, sink, *, batch, seq):
    t, d_attn = q.shape
    per = ATTN_BLOCKS_PER_STEP
    assert per == 2 and seq % (per * WINDOW) == 0
    ns = seq // (per * WINDOW)
    nb = seq // WINDOW
    rep = ATTN_HEADS // ATTN_KV_HEADS
    edge = lambda f: pl.BlockSpec((WINDOW, kv.shape[1]), f)
    table = lambda f: pl.BlockSpec((1, ATTN_KV_HEADS, 3 * WINDOW, rep * WINDOW), f)
    return pl.pallas_call(
        _attn_body,
        grid=(batch, ns),
        in_specs=[
            pl.BlockSpec((per * WINDOW, d_attn), lambda b, n: (b * ns + n, 0)),
            edge(lambda b, n: (b * nb + jnp.maximum(per * n - 1, 0), 0)),
            pl.BlockSpec((per * WINDOW, kv.shape[1]), lambda b, n: (b * ns + n, 0)),
            edge(lambda b, n: (b * nb + jnp.minimum(per * n + per, nb - 1), 0)),
            table(lambda b, n: (jnp.where(n == 0, 0, 1), 0, 0, 0)),
            table(lambda b, n: (jnp.where(n == ns - 1, 2, 1), 0, 0, 0)),
            _resident((ATTN_KV_HEADS, 1, rep * WINDOW)),
        ],
        out_specs=pl.BlockSpec((per * WINDOW, d_attn), lambda b, n: (b * ns + n, 0)),
        out_shape=jax.ShapeDtypeStruct((t, d_attn), BF16),
        scratch_shapes=[pltpu.VMEM((per, ATTN_KV_HEADS, 3 * WINDOW, rep * WINDOW), F32)],
        compiler_params=_params("parallel", "parallel"),
        name="attention",
    )(q, kv, kv, kv, bias, bias, sink)


def _attn_bias_tables():
    rep = ATTN_HEADS // ATTN_KV_HEADS
    kj = np.arange(3 * WINDOW)[:, None]
    qi = np.arange(WINDOW)[None, :]
    dist = np.abs(kj - WINDOW - qi).astype(np.float32)
    in_window = dist <= WINDOW
    slopes = 2.0 ** (-8.0 * np.arange(1, ATTN_HEADS + 1, dtype=np.float32) / ATTN_HEADS)
    alibi = -slopes.reshape(ATTN_KV_HEADS, rep)[:, None, :, None] * dist[None, :, None, :]
    cases = []
    for has_prev, has_next in ((False, True), (True, True), (True, False)):
        ok = in_window & ((kj >= WINDOW) | has_prev) & ((kj < 2 * WINDOW) | has_next)
        cases.append(np.where(ok[None, :, None, :], alibi, np.float32(MASKED)))
    out = np.stack(cases).astype(np.float32)
    return out.reshape(3, ATTN_KV_HEADS, 3 * WINDOW, rep * WINDOW)


def _out_mlp_body(x_ref, yf_ref, yb_ref, z_ref, ya_ref, nssm_ref, wo_ref, npost_ref, npre2_ref, wu_ref, wd_ref,
                  npost2_ref, o_ref, *, d_ssm, sub):
    for r0 in range(0, x_ref.shape[0], sub):
        rows = slice(r0, r0 + sub)
        y = yf_ref[rows, :].astype(F32) + yb_ref[rows, :].astype(F32)
        ys = _rmsnorm(y * _silu(z_ref[rows, :].astype(F32)), nssm_ref[...]).astype(BF16)
        mix = (jnp.dot(ys, wo_ref[:d_ssm, :], preferred_element_type=F32)
               + jnp.dot(ya_ref[rows, :], wo_ref[d_ssm:, :], preferred_element_type=F32))
        x1 = x_ref[rows, :] + _rmsnorm(mix, npost_ref[...])
        h = _rmsnorm(x1, npre2_ref[...]).astype(BF16)
        u = jnp.maximum(jnp.dot(h, wu_ref[...], preferred_element_type=F32), 0.0)
        f = jnp.dot((u * u).astype(BF16), wd_ref[...], preferred_element_type=F32)
        o_ref[rows, :] = x1 + _rmsnorm(f, npost2_ref[...])


def _out_mlp(x2, yf, yb, z, ya, nssm, wo, npost, npre2, wu, wd, npost2, *, tm, sub):
    t, d = x2.shape
    d_ssm = yf.shape[1]
    row = lambda width: pl.BlockSpec((tm, width), lambda i: (i, 0))
    return pl.pallas_call(
        functools.partial(_out_mlp_body, d_ssm=d_ssm, sub=sub),
        grid=(t // tm,),
        in_specs=[row(d), row(d_ssm), row(d_ssm), row(d_ssm), row(ya.shape[1]),
                  _resident((1, d_ssm)), _resident(wo.shape), _resident((1, d)),
                  _resident((1, d)), _resident(wu.shape), _resident(wd.shape), _resident((1, d))],
        out_specs=row(d),
        out_shape=jax.ShapeDtypeStruct((t, d), F32),
        compiler_params=_params("parallel"),
        name="out_mlp",
    )(x2, yf, yb, z, ya, nssm, wo, npost, npre2, wu, wd, npost2)


def _layer(x2, p, *, batch, seq):
    d = x2.shape[1]
    d_ssm = SSM_HEADS * SSM_HEAD_DIM
    d_xbc = d_ssm + 2 * SSM_GROUPS * SSM_STATE
    d_attn = ATTN_HEADS * ATTN_HEAD_DIM
    d_kv2 = 2 * ATTN_KV_HEADS * ATTN_HEAD_DIM
    n_dt = 2 * SSM_HEADS
    row = lambda v: v.reshape(1, -1).astype(F32)
    per_stream = lambda v: jnp.broadcast_to(v.reshape(n_dt, 1).astype(F32), (n_dt, CHUNK))

    w_in = p["w_in"]
    o_dt = d_ssm + d_xbc
    w_in = jnp.concatenate([w_in[:, :o_dt], w_in[:, o_dt + n_dt:], w_in[:, o_dt:o_dt + n_dt],
                            jnp.zeros((d, LANES - n_dt), w_in.dtype)], axis=1).astype(BF16)

    z, xc, bt, cc, acol, drow, q, kv = _in_proj(
        x2, row(p["norm_mix_pre"]), w_in, p["conv_w"].astype(F32), row(p["conv_b"]),
        per_stream(p["dt_bias"]), per_stream(p["a_log"]),
        seq=seq, tm=512, d_ssm=d_ssm, d_xbc=d_xbc, d_attn=d_attn, d_kv2=d_kv2)
    dskip = jnp.repeat(p["d_skip"].astype(F32), SSM_HEAD_DIM).reshape(1, d_ssm)
    yf, yb = _ssd_scan(xc, bt, cc, acol, drow, dskip, batch=batch, seq=seq)

    rep = ATTN_HEADS // ATTN_KV_HEADS
    sink = jnp.repeat(p["attn_sink"].astype(F32), WINDOW).reshape(ATTN_KV_HEADS, 1, rep * WINDOW)
    ya = _attention(q, kv, jnp.asarray(_attn_bias_tables()), sink, batch=batch, seq=seq)

    return _out_mlp(x2, yf, yb, z, ya, row(p["ssm_norm"]), p["w_out"].astype(BF16), row(p["norm_mix_post"]),
                    row(p["norm_mlp_pre"]), p["w_up"].astype(BF16), p["w_down"].astype(BF16),
                    row(p["norm_mlp_post"]), tm=512, sub=512)


def kernel(x, norm_mix_pre, w_in, conv_w, conv_b, dt_bias, a_log, d_skip, ssm_norm, attn_sink, w_out,
           norm_mix_post, norm_mlp_pre, w_up, w_down, norm_mlp_post):
    batch, seq, d = x.shape
    names = ("norm_mix_pre", "w_in", "conv_w", "conv_b", "dt_bias", "a_log", "d_skip", "ssm_norm", "attn_sink",
             "w_out", "norm_mix_post", "norm_mlp_pre", "w_up", "w_down", "norm_mlp_post")
    stacked = (norm_mix_pre, w_in, conv_w, conv_b, dt_bias, a_log, d_skip, ssm_norm, attn_sink, w_out,
               norm_mix_post, norm_mlp_pre, w_up, w_down, norm_mlp_post)
    x2 = x.reshape(batch * seq, d)
    for i in range(w_in.shape[0]):
        x2 = _layer(x2, {k: v[i] for k, v in zip(names, stacked)}, batch=batch, seq=seq)
    return x2.reshape(batch, seq, d)
```

```python
import functools

import jax
import jax.numpy as jnp
import numpy as np
from jax import lax
from jax.experimental import pallas as pl
from jax.experimental.pallas import tpu as pltpu

F32 = jnp.float32
BF16 = jnp.bfloat16

EPS = 1e-6
SSM_HEADS = 16
SSM_HEAD_DIM = 64
SSM_GROUPS = 2
SSM_STATE = 128
CONV_WIDTH = 5
CHUNK = 128
ATTN_HEADS = 16
ATTN_KV_HEADS = 4
ATTN_HEAD_DIM = 64
WINDOW = 128
LANES = 128
SUBLANES = 8
MASKED = -1e30
LOG2E = 1.4426950408889634
BF16_HUGE = 3e38
VMEM_LIMIT = 56 * 1024 * 1024

NT_DIMS = (((1,), (1,)), ((), ()))
TN_DIMS = (((0,), (0,)), ((), ()))


def _rmsnorm(x, w):
    return x * lax.rsqrt(jnp.mean(x * x, axis=-1, keepdims=True) + EPS) * w


def _silu(x):
    return 0.5 * x * (1.0 + jnp.tanh(0.5 * x))


def _params(*semantics):
    return pltpu.CompilerParams(dimension_semantics=semantics, vmem_limit_bytes=VMEM_LIMIT)


def _resident(shape):
    return pl.BlockSpec(shape, lambda *_: (0,) * len(shape), pipeline_mode=pl.Buffered(1))


def _in_proj_body(x_ref, xprev_ref, xnext_ref, g_ref, wa_ref, wb_ref, cw_ref, cb_ref, dtb_ref, alog_ref,
                  z_ref, xc_ref, bt_ref, cc_ref, acol_ref, drow_ref, q_ref, kv_ref,
                  ext_ref, *, tm, blocks_per_seq, d_ssm, d_xbc, d_attn, d_kv2):
    i = pl.program_id(0)
    halo = SUBLANES
    gs = SSM_GROUPS * SSM_STATE
    h = _rmsnorm(x_ref[...], g_ref[...]).astype(BF16)
    h_halo = _rmsnorm(jnp.concatenate([xprev_ref[...], xnext_ref[...]], axis=0), g_ref[...]).astype(BF16)
    he = jnp.concatenate([h, h_halo], axis=0)

    o_kv, o_dt = d_attn, d_attn + d_kv2
    first = i % blocks_per_seq == 0
    last = i % blocks_per_seq == blocks_per_seq - 1
    slab = 2 * LANES
    ext_rows = CHUNK + 2 * halo

    def xbc_slab(c0):
        xs = jnp.dot(he, wa_ref[:, d_ssm + c0:d_ssm + c0 + slab], preferred_element_type=F32)
        ext_ref[0:halo, c0:c0 + slab] = jnp.where(first, 0.0, xs[tm:tm + halo])
        ext_ref[halo:halo + tm, c0:c0 + slab] = xs[0:tm]
        ext_ref[halo + tm:, c0:c0 + slab] = jnp.where(last, 0.0, xs[tm + halo:])

    def out_slab(o_ref, w_ref, w0, c0, scale):
        acc = jnp.dot(h, w_ref[:, w0 + c0:w0 + c0 + slab], preferred_element_type=F32)
        o_ref[:, c0:c0 + slab] = (acc if scale == 1.0 else acc * scale).astype(BF16)

    def conv_strip(k, c0):
        r0 = k * CHUNK
        e = ext_ref[r0:r0 + ext_rows, c0:c0 + LANES]
        acc = cb_ref[:, c0:c0 + LANES] + cw_ref[2:3, c0:c0 + LANES] * e[halo:halo + CHUNK]
        for j in (0, 1, 3, 4):
            tap = pltpu.roll(e, (CONV_WIDTH // 2 - j) % ext_rows, 0)[halo:halo + CHUNK]
            acc = acc + cw_ref[j:j + 1, c0:c0 + LANES] * tap
        act = _silu(acc)
        if c0 < d_ssm:
            xc_ref[r0:r0 + CHUNK, c0:c0 + LANES] = act.astype(BF16)
        elif c0 < d_ssm + gs:
            g = (c0 - d_ssm) // SSM_STATE
            bt_ref[(k * SSM_GROUPS + g) * SSM_STATE:(k * SSM_GROUPS + g + 1) * SSM_STATE, :] = act.T.astype(BF16)
        else:
            cc_ref[r0:r0 + CHUNK, c0 - d_ssm - gs:c0 - d_ssm - gs + LANES] = act.astype(BF16)

    lane = lax.broadcasted_iota(jnp.int32, (2 * SSM_HEADS, CHUNK), 1)
    stream = lax.broadcasted_iota(jnp.int32, (2 * SSM_HEADS, CHUNK), 0)

    def decay_terms(dt_raw, k):
        r0 = k * CHUNK
        v = dt_raw[r0:r0 + CHUNK, :].T[0:2 * SSM_HEADS, :] + dtb_ref[...]
        dt = jnp.maximum(v, 0.0) + jnp.log1p(jnp.exp(-jnp.abs(v)))
        da = dt * (-jnp.exp(alog_ref[...]))
        pre = da
        sh = 1
        while sh < CHUNK:
            pre = pre + jnp.where(lane >= sh, pltpu.roll(pre, sh, 1), 0.0)
            sh *= 2
        total = jnp.broadcast_to(pre[:, CHUNK - 1:CHUNK], pre.shape)
        a = jnp.where(stream < SSM_HEADS, pre, total - pre + da)
        a2 = a * LOG2E
        grow = a2 - jnp.log2(dt)
        wrow = jnp.exp(total - a) * dt
        drow_ref[k * 4 * SSM_HEADS:(k + 1) * 4 * SSM_HEADS, :] = jnp.concatenate([grow, wrow], axis=0)
        a_pad = jnp.concatenate([a2, jnp.zeros((CHUNK - 2 * SSM_HEADS, CHUNK), F32)], axis=0)
        acol_ref[r0:r0 + CHUNK, :] = a_pad.T

    n_chunks = tm // CHUNK
    xbc_tasks = [functools.partial(xbc_slab, c0) for c0 in range(0, d_xbc, slab)]
    out_tasks = ([functools.partial(out_slab, z_ref, wa_ref, 0, c0, 1.0) for c0 in range(0, d_ssm, slab)]
                 + [functools.partial(out_slab, q_ref, wb_ref, 0, c0, ATTN_HEAD_DIM ** -0.5)
                    for c0 in range(0, d_attn, slab)]
                 + [functools.partial(out_slab, kv_ref, wb_ref, o_kv, c0, 1.0) for c0 in range(0, d_kv2, slab)])
    mxu_tasks = [xbc_tasks[0]]
    for n_x in range(1, len(xbc_tasks)):
        mxu_tasks += [xbc_tasks[n_x], out_tasks.pop(0)]
    mxu_tasks += out_tasks
    strips = [(k, c0) for c0 in range(0, d_xbc, LANES) for k in range(n_chunks)]
    per_task = -(-len(strips) // (len(mxu_tasks) - 1))
    dt_raw = jnp.dot(h, wb_ref[:, o_dt:o_dt + LANES], preferred_element_type=F32)
    decay_chunks = list(range(n_chunks))
    ready_cols = 0
    for task in mxu_tasks:
        task()
        if decay_chunks:
            decay_terms(dt_raw, decay_chunks.pop(0))
        for _ in range(per_task):
            if strips and strips[0][1] < ready_cols:
                conv_strip(*strips.pop(0))
        ready_cols += slab * (task.func is xbc_slab)
    for strip in strips:
        conv_strip(*strip)


def _in_proj(x2, gain, wa, wb, conv_w, conv_b, dt_bias, a_log, *, seq, tm, d_ssm, d_xbc, d_attn, d_kv2):
    t, d = x2.shape
    halo = SUBLANES
    nsub = tm // halo
    nhalo = t // halo
    chunks = tm // CHUNK
    gs = SSM_GROUPS * SSM_STATE
    row = lambda rows, width: pl.BlockSpec((rows, width), lambda i: (i, 0))
    out_shape = [
        jax.ShapeDtypeStruct((t, d_ssm), BF16),
        jax.ShapeDtypeStruct((t, d_ssm), BF16),
        jax.ShapeDtypeStruct((t // CHUNK * gs, CHUNK), BF16),
        jax.ShapeDtypeStruct((t, gs), BF16),
        jax.ShapeDtypeStruct((t, LANES), F32),
        jax.ShapeDtypeStruct((t // CHUNK * 4 * SSM_HEADS, CHUNK), F32),
        jax.ShapeDtypeStruct((t, d_attn), BF16),
        jax.ShapeDtypeStruct((t, d_kv2), BF16),
    ]
    return pl.pallas_call(
        functools.partial(_in_proj_body, tm=tm, blocks_per_seq=seq // tm, d_ssm=d_ssm, d_xbc=d_xbc,
                          d_attn=d_attn, d_kv2=d_kv2),
        grid=(t // tm,),
        in_specs=[row(tm, d),
                  pl.BlockSpec((halo, d), lambda i: (jnp.maximum(i * nsub - 1, 0), 0)),
                  pl.BlockSpec((halo, d), lambda i: (jnp.minimum((i + 1) * nsub, nhalo - 1), 0)),
                  _resident((1, d)),
                  pl.BlockSpec((d, d_ssm + d_xbc), lambda i: (0, 0), pipeline_mode=pl.Buffered(1)),
                  _resident(wb.shape), _resident((CONV_WIDTH, d_xbc)), _resident((1, d_xbc)),
                  _resident((2 * SSM_HEADS, CHUNK)), _resident((2 * SSM_HEADS, CHUNK))],
        out_specs=[row(tm, d_ssm), row(tm, d_ssm), row(chunks * gs, CHUNK), row(tm, gs), row(tm, LANES),
                   row(chunks * 4 * SSM_HEADS, CHUNK), row(tm, d_attn), row(tm, d_kv2)],
        out_shape=out_shape,
        scratch_shapes=[pltpu.VMEM((tm + 2 * halo, d_xbc), F32)],
        compiler_params=_params("parallel"),
        name="in_proj",
    )(x2, x2, x2, gain, wa, wb, conv_w, conv_b, dt_bias, a_log)


def _ssd_scan_body(xf_ref, btf_ref, ccf_ref, acf_ref, drf_ref, xb_ref, btb_ref, ccb_ref, acb_ref, drb_ref,
                   dskip_ref, yf_ref, yb_ref, stf_ref, stb_ref):
    @pl.when(pl.program_id(1) == 0)
    def _():
        stf_ref[...] = jnp.zeros_like(stf_ref)
        stb_ref[...] = jnp.zeros_like(stb_ref)

    gw = (SSM_HEADS // SSM_GROUPS) * SSM_HEAD_DIM
    pairs_per_group = gw // LANES
    row = lax.broadcasted_iota(jnp.int32, (CHUNK, CHUNK), 0)
    lane = lax.broadcasted_iota(jnp.int32, (CHUNK, CHUNK), 1)
    first_head = lane < SSM_HEAD_DIM
    keep_first = jnp.where(first_head, 1.0, 0.0).astype(BF16)
    keep_second = jnp.where(first_head, 0.0, 1.0).astype(BF16)

    def visit(x_ref, bt_ref, cc_ref, ac, dr, st_ref, y_ref, direction):
        off = direction * SSM_HEADS
        inside = row >= lane if direction == 0 else row <= lane
        edge = CHUNK - 1 if direction == 0 else 0
        for g in range(SSM_GROUPS):
            btg = bt_ref[g * SSM_STATE:(g + 1) * SSM_STATE, :]
            cg = cc_ref[:, g * SSM_STATE:(g + 1) * SSM_STATE]
            cb = jnp.dot(cg, btg, preferred_element_type=F32)
            cbm = jnp.where(inside, cb, 0.0).astype(BF16)
            btf = btg.astype(F32)
            st = st_ref[:, g * gw:(g + 1) * gw]
            y_off = jnp.dot(cg, st.astype(BF16), preferred_element_type=F32)
            for jj in range(pairs_per_group):
                j = g * pairs_per_group + jj
                sl = slice(j * LANES, (j + 1) * LANES)
                decays, weighted_bt, cols = [], [], []
                for h in (2 * j, 2 * j + 1):
                    col = jnp.broadcast_to(ac[:, off + h:off + h + 1], (CHUNK, CHUNK))
                    grow = dr[off + h:off + h + 1, :]
                    wrow = dr[2 * SSM_HEADS + off + h:2 * SSM_HEADS + off + h + 1, :]
                    decay = jnp.minimum(jnp.exp2(col - grow).astype(BF16), BF16_HUGE)
                    decays.append(decay * cbm)
                    weighted_bt.append((btf * wrow).astype(BF16))
                    cols.append(col)
                lhs = jnp.concatenate([jnp.concatenate(decays, axis=1), jnp.concatenate(weighted_bt, axis=1)], axis=0)
                xp = x_ref[:, sl]
                rhs = jnp.concatenate([xp * keep_first, xp * keep_second], axis=0)
                out = jnp.dot(lhs, rhs, preferred_element_type=F32)
                ea = jnp.exp2(jnp.where(first_head, cols[0], cols[1]))
                y = out[0:CHUNK] + ea * y_off[:, jj * LANES:(jj + 1) * LANES]
                if direction == 0:
                    y = y + xp.astype(F32) * dskip_ref[:, sl]
                y_ref[:, sl] = y.astype(BF16)
                st_ref[:, sl] = st[:, jj * LANES:(jj + 1) * LANES] * ea[edge:edge + 1, :] + out[CHUNK:]

    visit(xf_ref, btf_ref, ccf_ref, acf_ref[...], drf_ref[...], stf_ref, yf_ref, 0)
    visit(xb_ref, btb_ref, ccb_ref, acb_ref[...], drb_ref[...], stb_ref, yb_ref, 1)


def _ssd_scan(xc, bt, cc, acol, drow, dskip, *, batch, seq):
    t, d_ssm = xc.shape
    nc = seq // CHUNK
    gs = SSM_GROUPS * SSM_STATE
    fwd = lambda rows, width: pl.BlockSpec((rows, width), lambda b, c: (b * nc + c, 0))
    bwd = lambda rows, width: pl.BlockSpec((rows, width), lambda b, c: (b * nc + nc - 1 - c, 0))
    operands = lambda spec: [spec(CHUNK, d_ssm), spec(gs, CHUNK), spec(CHUNK, gs), spec(CHUNK, LANES),
                             spec(4 * SSM_HEADS, CHUNK)]
    return pl.pallas_call(
        _ssd_scan_body,
        grid=(batch, nc),
        in_specs=operands(fwd) + operands(bwd) + [_resident((1, d_ssm))],
        out_specs=[fwd(CHUNK, d_ssm), bwd(CHUNK, d_ssm)],
        out_shape=[jax.ShapeDtypeStruct((t, d_ssm), BF16), jax.ShapeDtypeStruct((t, d_ssm), BF16)],
        scratch_shapes=[pltpu.VMEM((SSM_STATE, d_ssm), F32), pltpu.VMEM((SSM_STATE, d_ssm), F32)],
        compiler_params=_params("arbitrary", "arbitrary"),
        name="ssd_scan",
    )(xc, bt, cc, acol, drow, xc, bt, cc, acol, drow, dskip)


ATTN_BLOCKS_PER_STEP = 4


def _attn_body(q_ref, kvp_ref, kvc_ref, kvn_ref, bias_lo_ref, bias_mid_ref, bias_hi_ref, sink_ref, o_ref, s_ref):
    rep = ATTN_HEADS // ATTN_KV_HEADS
    d_kv = ATTN_KV_HEADS * ATTN_HEAD_DIM
    kv = jnp.concatenate([kvp_ref[...], kvc_ref[...], kvn_ref[...]], axis=0)
    bias_refs = (bias_lo_ref,) + (bias_mid_ref,) * (ATTN_BLOCKS_PER_STEP - 2) + (bias_hi_ref,)
    units = [(blk, g) for blk in range(ATTN_BLOCKS_PER_STEP) for g in range(ATTN_KV_HEADS)]
    row_max = {}
    for blk, g in units:
        rows = slice(blk * WINDOW, (blk + 1) * WINDOW)
        kg = kv[blk * WINDOW:(blk + 3) * WINDOW, g * ATTN_HEAD_DIM:(g + 1) * ATTN_HEAD_DIM]
        qg = jnp.concatenate([q_ref[rows, (g * rep + r) * ATTN_HEAD_DIM:(g * rep + r + 1) * ATTN_HEAD_DIM]
                              for r in range(rep)], axis=0)
        s = lax.dot_general(kg, qg, NT_DIMS, preferred_element_type=F32) + bias_refs[blk][0, g]
        s_ref[blk, g] = s
        row_max[blk, g] = jnp.maximum(jnp.max(s, axis=0, keepdims=True), sink_ref[g])
    for blk, g in units:
        rows = slice(blk * WINDOW, (blk + 1) * WINDOW)
        vg = kv[blk * WINDOW:(blk + 3) * WINDOW, d_kv + g * ATTN_HEAD_DIM:d_kv + (g + 1) * ATTN_HEAD_DIM]
        m = row_max[blk, g]
        p = jnp.exp(s_ref[blk, g] - m)
        denom = jnp.sum(p, axis=0, keepdims=True) + jnp.exp(sink_ref[g] - m)
        o = lax.dot_general(vg, p.astype(BF16), TN_DIMS, preferred_element_type=F32) / denom
        for r in range(0, rep, 2):
            two = jnp.concatenate([o[:, r * WINDOW:(r + 1) * WINDOW], o[:, (r + 1) * WINDOW:(r + 2) * WINDOW]], axis=0)
            lo = (g * rep + r) * ATTN_HEAD_DIM
            o_ref[rows, lo:lo + 2 * ATTN_HEAD_DIM] = two.T.astype(BF16)


def _attention(q, kv, bias, sink, *, batch, seq):
    t, d_attn = q.shape
    per = ATTN_BLOCKS_PER_STEP
    assert per >= 2 and seq % (per * WINDOW) == 0
    ns = seq // (per * WINDOW)
    nb = seq // WINDOW
    rep = ATTN_HEADS // ATTN_KV_HEADS
    edge = lambda f: pl.BlockSpec((WINDOW, kv.shape[1]), f)
    table = lambda f: pl.BlockSpec((1, ATTN_KV_HEADS, 3 * WINDOW, rep * WINDOW), f)
    return pl.pallas_call(
        _attn_body,
        grid=(batch, ns),
        in_specs=[
            pl.BlockSpec((per * WINDOW, d_attn), lambda b, n: (b * ns + n, 0)),
            edge(lambda b, n: (b * nb + jnp.maximum(per * n - 1, 0), 0)),
            pl.BlockSpec((per * WINDOW, kv.shape[1]), lambda b, n: (b * ns + n, 0)),
            edge(lambda b, n: (b * nb + jnp.minimum(per * n + per, nb - 1), 0)),
            table(lambda b, n: (jnp.where(n == 0, 0, 1), 0, 0, 0)),
            pl.BlockSpec((1, ATTN_KV_HEADS, 3 * WINDOW, rep * WINDOW), lambda b, n: (1, 0, 0, 0),
                         pipeline_mode=pl.Buffered(1)),
            table(lambda b, n: (jnp.where(n == ns - 1, 2, 1), 0, 0, 0)),
            _resident((ATTN_KV_HEADS, 1, rep * WINDOW)),
        ],
        out_specs=pl.BlockSpec((per * WINDOW, d_attn), lambda b, n: (b * ns + n, 0)),
        out_shape=jax.ShapeDtypeStruct((t, d_attn), BF16),
        scratch_shapes=[pltpu.VMEM((per, ATTN_KV_HEADS, 3 * WINDOW, rep * WINDOW), F32)],
        compiler_params=_params("parallel", "parallel"),
        name="attention",
    )(q, kv, kv, kv, bias, bias, bias, sink)


def _attn_bias_tables():
    rep = ATTN_HEADS // ATTN_KV_HEADS
    kj = np.arange(3 * WINDOW)[:, None]
    qi = np.arange(WINDOW)[None, :]
    dist = np.abs(kj - WINDOW - qi).astype(np.float32)
    in_window = dist <= WINDOW
    slopes = 2.0 ** (-8.0 * np.arange(1, ATTN_HEADS + 1, dtype=np.float32) / ATTN_HEADS)
    alibi = -slopes.reshape(ATTN_KV_HEADS, rep)[:, None, :, None] * dist[None, :, None, :]
    cases = []
    for has_prev, has_next in ((False, True), (True, True), (True, False)):
        ok = in_window & ((kj >= WINDOW) | has_prev) & ((kj < 2 * WINDOW) | has_next)
        cases.append(np.where(ok[None, :, None, :], alibi, np.float32(MASKED)))
    out = np.stack(cases).astype(np.float32)
    return out.reshape(3, ATTN_KV_HEADS, 3 * WINDOW, rep * WINDOW)


def _out_mlp_body(x_ref, yf_ref, yb_ref, z_ref, ya_ref, nssm_ref, wo_ref, npost_ref, npre2_ref, wu_ref, wd_ref,
                  npost2_ref, o_ref, *, d_ssm):
    y = yf_ref[...].astype(F32) + yb_ref[...].astype(F32)
    ys = _rmsnorm(y * _silu(z_ref[...].astype(F32)), nssm_ref[...]).astype(BF16)
    mix = (jnp.dot(ys, wo_ref[:d_ssm, :], preferred_element_type=F32)
           + jnp.dot(ya_ref[...], wo_ref[d_ssm:, :], preferred_element_type=F32))
    x1 = x_ref[...] + _rmsnorm(mix, npost_ref[...])
    h = _rmsnorm(x1, npre2_ref[...]).astype(BF16)
    u = jnp.maximum(jnp.dot(h, wu_ref[...], preferred_element_type=F32), 0.0)
    f = jnp.dot((u * u).astype(BF16), wd_ref[...], preferred_element_type=F32)
    o_ref[...] = x1 + _rmsnorm(f, npost2_ref[...])


def _out_mlp(x2, yf, yb, z, ya, nssm, wo, npost, npre2, wu, wd, npost2, *, tm):
    t, d = x2.shape
    d_ssm = yf.shape[1]
    row = lambda width: pl.BlockSpec((tm, width), lambda i: (i, 0))
    return pl.pallas_call(
        functools.partial(_out_mlp_body, d_ssm=d_ssm),
        grid=(t // tm,),
        in_specs=[row(d), row(d_ssm), row(d_ssm), row(d_ssm), row(ya.shape[1]),
                  _resident((1, d_ssm)), _resident(wo.shape), _resident((1, d)),
                  _resident((1, d)), _resident(wu.shape), _resident(wd.shape), _resident((1, d))],
        out_specs=row(d),
        out_shape=jax.ShapeDtypeStruct((t, d), F32),
        compiler_params=_params("parallel"),
        name="out_mlp",
    )(x2, yf, yb, z, ya, nssm, wo, npost, npre2, wu, wd, npost2)


def _layer(x2, p, *, batch, seq):
    d = x2.shape[1]
    d_ssm = SSM_HEADS * SSM_HEAD_DIM
    d_xbc = d_ssm + 2 * SSM_GROUPS * SSM_STATE
    d_attn = ATTN_HEADS * ATTN_HEAD_DIM
    d_kv2 = 2 * ATTN_KV_HEADS * ATTN_HEAD_DIM
    n_dt = 2 * SSM_HEADS
    row = lambda v: v.reshape(1, -1).astype(F32)
    per_stream = lambda v: jnp.broadcast_to(v.reshape(n_dt, 1).astype(F32), (n_dt, CHUNK))

    wa = p["w_in"].astype(BF16)
    o_dt = d_ssm + d_xbc
    wb = jnp.concatenate([wa[:, o_dt + n_dt:], wa[:, o_dt:o_dt + n_dt], jnp.zeros((d, LANES - n_dt), BF16)], axis=1)

    z, xc, bt, cc, acol, drow, q, kv = _in_proj(
        x2, row(p["norm_mix_pre"]), wa, wb, p["conv_w"].astype(F32), row(p["conv_b"]),
        per_stream(p["dt_bias"]), per_stream(p["a_log"]),
        seq=seq, tm=512, d_ssm=d_ssm, d_xbc=d_xbc, d_attn=d_attn, d_kv2=d_kv2)
    dskip = jnp.repeat(p["d_skip"].astype(F32), SSM_HEAD_DIM).reshape(1, d_ssm)
    yf, yb = _ssd_scan(xc, bt, cc, acol, drow, dskip, batch=batch, seq=seq)

    rep = ATTN_HEADS // ATTN_KV_HEADS
    sink = jnp.repeat(p["attn_sink"].astype(F32), WINDOW).reshape(ATTN_KV_HEADS, 1, rep * WINDOW)
    ya = _attention(q, kv, jnp.asarray(_attn_bias_tables()), sink, batch=batch, seq=seq)

    return _out_mlp(x2, yf, yb, z, ya, row(p["ssm_norm"]), p["w_out"].astype(BF16), row(p["norm_mix_post"]),
                    row(p["norm_mlp_pre"]), p["w_up"].astype(BF16), p["w_down"].astype(BF16),
                    row(p["norm_mlp_post"]), tm=512)


def kernel(x, norm_mix_pre, w_in, conv_w, conv_b, dt_bias, a_log, d_skip, ssm_norm, attn_sink, w_out,
           norm_mix_post, norm_mlp_pre, w_up, w_down, norm_mlp_post):
    batch, seq, d = x.shape
    names = ("norm_mix_pre", "w_in", "conv_w", "conv_b", "dt_bias", "a_log", "d_skip", "ssm_norm", "attn_sink",
             "w_out", "norm_mix_post", "norm_mlp_pre", "w_up", "w_down", "norm_mlp_post")
    stacked = (norm_mix_pre, w_in, conv_w, conv_b, dt_bias, a_log, d_skip, ssm_norm, attn_sink, w_out,
               norm_mix_post, norm_mlp_pre, w_up, w_down, norm_mlp_post)
    x2 = x.reshape(batch * seq, d)
    for i in range(w_in.shape[0]):
        x2 = _layer(x2, {k: v[i] for k, v in zip(names, stacked)}, batch=batch, seq=seq)
    return x2.reshape(batch, seq, d)
```

```python
import functools

import jax
import jax.numpy as jnp
import numpy as np
from jax import lax
from jax.experimental import pallas as pl
from jax.experimental.pallas import tpu as pltpu

F32 = jnp.float32
BF16 = jnp.bfloat16

EPS = 1e-6
SSM_HEADS = 16
SSM_HEAD_DIM = 64
SSM_GROUPS = 2
SSM_STATE = 128
CONV_WIDTH = 5
CHUNK = 128
ATTN_HEADS = 16
ATTN_KV_HEADS = 4
ATTN_HEAD_DIM = 64
WINDOW = 128
LANES = 128
SUBLANES = 8
MASKED = -1e30
LOG2E = 1.4426950408889634
BF16_HUGE = 3e38
VMEM_LIMIT = 56 * 1024 * 1024

NT_DIMS = (((1,), (1,)), ((), ()))
TN_DIMS = (((0,), (0,)), ((), ()))


def _rmsnorm(x, w):
    return x * lax.rsqrt(jnp.mean(x * x, axis=-1, keepdims=True) + EPS) * w


def _silu(x):
    return 0.5 * x * (1.0 + jnp.tanh(0.5 * x))


def _params(*semantics):
    return pltpu.CompilerParams(dimension_semantics=semantics, vmem_limit_bytes=VMEM_LIMIT)


def _resident(shape):
    return pl.BlockSpec(shape, lambda *_: (0,) * len(shape), pipeline_mode=pl.Buffered(1))


def _in_proj_body(x_ref, xprev_ref, xnext_ref, g_ref, wa_ref, wb_ref, cw_ref, cb_ref, dtb_ref, alog_ref,
                  z_ref, xc_ref, bt_ref, cc_ref, acol_ref, drow_ref, q_ref, kv_ref,
                  ext_ref, *, tm, blocks_per_seq, d_ssm, d_xbc, d_attn, d_kv2):
    i = pl.program_id(0)
    halo = SUBLANES
    gs = SSM_GROUPS * SSM_STATE
    h = _rmsnorm(x_ref[...], g_ref[...]).astype(BF16)
    h_halo = _rmsnorm(jnp.concatenate([xprev_ref[...], xnext_ref[...]], axis=0), g_ref[...]).astype(BF16)
    he = jnp.concatenate([h, h_halo], axis=0)

    o_kv, o_dt = d_attn, d_attn + d_kv2
    first = i % blocks_per_seq == 0
    last = i % blocks_per_seq == blocks_per_seq - 1
    slab = 2 * LANES
    ext_rows = CHUNK + 2 * halo

    def xbc_slab(c0):
        xs = jnp.dot(he, wa_ref[:, d_ssm + c0:d_ssm + c0 + slab], preferred_element_type=F32)
        ext_ref[0:halo, c0:c0 + slab] = jnp.where(first, 0.0, xs[tm:tm + halo])
        ext_ref[halo:halo + tm, c0:c0 + slab] = xs[0:tm]
        ext_ref[halo + tm:2 * halo + tm, c0:c0 + slab] = jnp.where(last, 0.0, xs[tm + halo:])

    spare = slice(tm + 2 * halo, tm + 2 * halo + SUBLANES)
    unresolved_zero = pl.multiple_of(jnp.minimum(i, 0), SUBLANES)

    def out_slab(o_ref, w_ref, w0, c0, scale):
        acc = jnp.dot(h, w_ref[:, w0 + c0:w0 + c0 + slab], preferred_element_type=F32)
        o_ref[:, c0:c0 + slab] = (acc if scale == 1.0 else acc * scale).astype(BF16)
        ext_ref[spare, 0:LANES] = acc[tm - SUBLANES:tm, 0:LANES]

    def conv_strip(k, c0):
        r0 = k * CHUNK
        e = ext_ref[pl.ds(r0 + unresolved_zero, ext_rows), c0:c0 + LANES]
        acc = cb_ref[:, c0:c0 + LANES] + cw_ref[2:3, c0:c0 + LANES] * e[halo:halo + CHUNK]
        for j in (0, 1, 3, 4):
            tap = pltpu.roll(e, (CONV_WIDTH // 2 - j) % ext_rows, 0)[halo:halo + CHUNK]
            acc = acc + cw_ref[j:j + 1, c0:c0 + LANES] * tap
        act = _silu(acc)
        if c0 < d_ssm:
            xc_ref[r0:r0 + CHUNK, c0:c0 + LANES] = act.astype(BF16)
        elif c0 < d_ssm + gs:
            g = (c0 - d_ssm) // SSM_STATE
            bt_ref[(k * SSM_GROUPS + g) * SSM_STATE:(k * SSM_GROUPS + g + 1) * SSM_STATE, :] = act.T.astype(BF16)
        else:
            cc_ref[r0:r0 + CHUNK, c0 - d_ssm - gs:c0 - d_ssm - gs + LANES] = act.astype(BF16)

    lane = lax.broadcasted_iota(jnp.int32, (2 * SSM_HEADS, CHUNK), 1)
    stream = lax.broadcasted_iota(jnp.int32, (2 * SSM_HEADS, CHUNK), 0)

    def decay_terms(dt_raw, k):
        r0 = k * CHUNK
        v = dt_raw[r0:r0 + CHUNK, :].T[0:2 * SSM_HEADS, :] + dtb_ref[...]
        dt = jnp.maximum(v, 0.0) + jnp.log1p(jnp.exp(-jnp.abs(v)))
        da = dt * (-jnp.exp(alog_ref[...]))
        pre = da
        sh = 1
        while sh < CHUNK:
            pre = pre + jnp.where(lane >= sh, pltpu.roll(pre, sh, 1), 0.0)
            sh *= 2
        total = jnp.broadcast_to(pre[:, CHUNK - 1:CHUNK], pre.shape)
        a = jnp.where(stream < SSM_HEADS, pre, total - pre + da)
        a2 = a * LOG2E
        grow = a2 - jnp.log2(dt)
        wrow = jnp.exp(total - a) * dt
        drow_ref[k * 4 * SSM_HEADS:(k + 1) * 4 * SSM_HEADS, :] = jnp.concatenate([grow, wrow], axis=0)
        a_pad = jnp.concatenate([a2, jnp.zeros((CHUNK - 2 * SSM_HEADS, CHUNK), F32)], axis=0)
        acol_ref[r0:r0 + CHUNK, :] = a_pad.T

    n_chunks = tm // CHUNK
    mxu_tasks = ([functools.partial(xbc_slab, c0) for c0 in range(0, d_xbc, slab)]
                 + [functools.partial(out_slab, z_ref, wa_ref, 0, c0, 1.0) for c0 in range(0, d_ssm, slab)]
                 + [functools.partial(out_slab, q_ref, wb_ref, 0, c0, ATTN_HEAD_DIM ** -0.5)
                    for c0 in range(0, d_attn, slab)]
                 + [functools.partial(out_slab, kv_ref, wb_ref, o_kv, c0, 1.0) for c0 in range(0, d_kv2, slab)])
    strips = [(k, c0) for c0 in range(0, d_xbc, LANES) for k in range(n_chunks)]
    per_task = -(-len(strips) // (len(mxu_tasks) - 4))
    dt_raw = jnp.dot(h, wb_ref[:, o_dt:o_dt + LANES], preferred_element_type=F32)
    decay_chunks = list(range(n_chunks))
    ready_cols = 0
    for task in mxu_tasks:
        task()
        ready_cols += slab * (task.func is xbc_slab)
        if decay_chunks:
            decay_terms(dt_raw, decay_chunks.pop(0))
        for _ in range(per_task):
            if strips and strips[0][1] < ready_cols:
                conv_strip(*strips.pop(0))
    for strip in strips:
        conv_strip(*strip)


def _in_proj(x2, gain, wa, wb, conv_w, conv_b, dt_bias, a_log, *, seq, tm, d_ssm, d_xbc, d_attn, d_kv2):
    t, d = x2.shape
    halo = SUBLANES
    nsub = tm // halo
    nhalo = t // halo
    chunks = tm // CHUNK
    gs = SSM_GROUPS * SSM_STATE
    row = lambda rows, width: pl.BlockSpec((rows, width), lambda i: (i, 0))
    out_shape = [
        jax.ShapeDtypeStruct((t, d_ssm), BF16),
        jax.ShapeDtypeStruct((t, d_ssm), BF16),
        jax.ShapeDtypeStruct((t // CHUNK * gs, CHUNK), BF16),
        jax.ShapeDtypeStruct((t, gs), BF16),
        jax.ShapeDtypeStruct((t, LANES), F32),
        jax.ShapeDtypeStruct((t // CHUNK * 4 * SSM_HEADS, CHUNK), F32),
        jax.ShapeDtypeStruct((t, d_attn), BF16),
        jax.ShapeDtypeStruct((t, d_kv2), BF16),
    ]
    return pl.pallas_call(
        functools.partial(_in_proj_body, tm=tm, blocks_per_seq=seq // tm, d_ssm=d_ssm, d_xbc=d_xbc,
                          d_attn=d_attn, d_kv2=d_kv2),
        grid=(t // tm,),
        in_specs=[row(tm, d),
                  pl.BlockSpec((halo, d), lambda i: (jnp.maximum(i * nsub - 1, 0), 0)),
                  pl.BlockSpec((halo, d), lambda i: (jnp.minimum((i + 1) * nsub, nhalo - 1), 0)),
                  _resident((1, d)),
                  pl.BlockSpec((d, d_ssm + d_xbc), lambda i: (0, 0), pipeline_mode=pl.Buffered(1)),
                  _resident(wb.shape), _resident((CONV_WIDTH, d_xbc)), _resident((1, d_xbc)),
                  _resident((2 * SSM_HEADS, CHUNK)), _resident((2 * SSM_HEADS, CHUNK))],
        out_specs=[row(tm, d_ssm), row(tm, d_ssm), row(chunks * gs, CHUNK), row(tm, gs), row(tm, LANES),
                   row(chunks * 4 * SSM_HEADS, CHUNK), row(tm, d_attn), row(tm, d_kv2)],
        out_shape=out_shape,
        scratch_shapes=[pltpu.VMEM((tm + 2 * halo + SUBLANES, d_xbc), F32)],
        compiler_params=_params("parallel"),
        name="in_proj",
    )(x2, x2, x2, gain, wa, wb, conv_w, conv_b, dt_bias, a_log)


def _ssd_scan_body(xf_ref, btf_ref, ccf_ref, acf_ref, drf_ref, xb_ref, btb_ref, ccb_ref, acb_ref, drb_ref,
                   dskip_ref, yf_ref, yb_ref, stf_ref, stb_ref):
    @pl.when(pl.program_id(1) == 0)
    def _():
        stf_ref[...] = jnp.zeros_like(stf_ref)
        stb_ref[...] = jnp.zeros_like(stb_ref)

    gw = (SSM_HEADS // SSM_GROUPS) * SSM_HEAD_DIM
    pairs_per_group = gw // LANES
    row = lax.broadcasted_iota(jnp.int32, (CHUNK, CHUNK), 0)
    lane = lax.broadcasted_iota(jnp.int32, (CHUNK, CHUNK), 1)
    first_head = lane < SSM_HEAD_DIM
    keep_first = jnp.where(first_head, 1.0, 0.0).astype(BF16)
    keep_second = jnp.where(first_head, 0.0, 1.0).astype(BF16)

    def visit(x_ref, bt_ref, cc_ref, ac, dr, st_ref, y_ref, direction):
        off = direction * SSM_HEADS
        inside = row >= lane if direction == 0 else row <= lane
        edge = CHUNK - 1 if direction == 0 else 0
        for g in range(SSM_GROUPS):
            btg = bt_ref[g * SSM_STATE:(g + 1) * SSM_STATE, :]
            cg = cc_ref[:, g * SSM_STATE:(g + 1) * SSM_STATE]
            cb = jnp.dot(cg, btg, preferred_element_type=F32)
            cbm = jnp.where(inside, cb, 0.0).astype(BF16)
            btf = btg.astype(F32)
            st = st_ref[:, g * gw:(g + 1) * gw]
            y_off = jnp.dot(cg, st.astype(BF16), preferred_element_type=F32)
            for jj in range(pairs_per_group):
                j = g * pairs_per_group + jj
                sl = slice(j * LANES, (j + 1) * LANES)
                decays, weighted_bt, cols = [], [], []
                for h in (2 * j, 2 * j + 1):
                    col = jnp.broadcast_to(ac[:, off + h:off + h + 1], (CHUNK, CHUNK))
                    grow = dr[off + h:off + h + 1, :]
                    wrow = dr[2 * SSM_HEADS + off + h:2 * SSM_HEADS + off + h + 1, :]
                    decay = jnp.minimum(jnp.exp2(col - grow).astype(BF16), BF16_HUGE)
                    decays.append(decay * cbm)
                    weighted_bt.append((btf * wrow).astype(BF16))
                    cols.append(col)
                lhs = jnp.concatenate([jnp.concatenate(decays, axis=1), jnp.concatenate(weighted_bt, axis=1)], axis=0)
                xp = x_ref[:, sl]
                rhs = jnp.concatenate([xp * keep_first, xp * keep_second], axis=0)
                out = jnp.dot(lhs, rhs, preferred_element_type=F32)
                ea = jnp.exp2(jnp.where(first_head, cols[0], cols[1]))
                y = out[0:CHUNK] + ea * y_off[:, jj * LANES:(jj + 1) * LANES]
                if direction == 0:
                    y = y + xp.astype(F32) * dskip_ref[:, sl]
                y_ref[:, sl] = y.astype(BF16)
                st_ref[:, sl] = st[:, jj * LANES:(jj + 1) * LANES] * ea[edge:edge + 1, :] + out[CHUNK:]

    visit(xf_ref, btf_ref, ccf_ref, acf_ref[...], drf_ref[...], stf_ref, yf_ref, 0)
    visit(xb_ref, btb_ref, ccb_ref, acb_ref[...], drb_ref[...], stb_ref, yb_ref, 1)


def _ssd_scan(xc, bt, cc, acol, drow, dskip, *, batch, seq):
    t, d_ssm = xc.shape
    nc = seq // CHUNK
    gs = SSM_GROUPS * SSM_STATE
    fwd = lambda rows, width: pl.BlockSpec((rows, width), lambda b, c: (b * nc + c, 0))
    bwd = lambda rows, width: pl.BlockSpec((rows, width), lambda b, c: (b * nc + nc - 1 - c, 0))
    operands = lambda spec: [spec(CHUNK, d_ssm), spec(gs, CHUNK), spec(CHUNK, gs), spec(CHUNK, LANES),
                             spec(4 * SSM_HEADS, CHUNK)]
    return pl.pallas_call(
        _ssd_scan_body,
        grid=(batch, nc),
        in_specs=operands(fwd) + operands(bwd) + [_resident((1, d_ssm))],
        out_specs=[fwd(CHUNK, d_ssm), bwd(CHUNK, d_ssm)],
        out_shape=[jax.ShapeDtypeStruct((t, d_ssm), BF16), jax.ShapeDtypeStruct((t, d_ssm), BF16)],
        scratch_shapes=[pltpu.VMEM((SSM_STATE, d_ssm), F32), pltpu.VMEM((SSM_STATE, d_ssm), F32)],
        compiler_params=_params("arbitrary", "arbitrary"),
        name="ssd_scan",
    )(xc, bt, cc, acol, drow, xc, bt, cc, acol, drow, dskip)


ATTN_BLOCKS_PER_STEP = 4


def _attn_body(q_ref, kvp_ref, kvc_ref, kvn_ref, bias_lo_ref, bias_mid_ref, bias_hi_ref, sink_ref, o_ref, s_ref):
    rep = ATTN_HEADS // ATTN_KV_HEADS
    d_kv = ATTN_KV_HEADS * ATTN_HEAD_DIM
    kv = jnp.concatenate([kvp_ref[...], kvc_ref[...], kvn_ref[...]], axis=0)
    bias_refs = (bias_lo_ref,) + (bias_mid_ref,) * (ATTN_BLOCKS_PER_STEP - 2) + (bias_hi_ref,)
    units = [(blk, g) for blk in range(ATTN_BLOCKS_PER_STEP) for g in range(ATTN_KV_HEADS)]
    row_max = {}
    for blk, g in units:
        rows = slice(blk * WINDOW, (blk + 1) * WINDOW)
        kg = kv[blk * WINDOW:(blk + 3) * WINDOW, g * ATTN_HEAD_DIM:(g + 1) * ATTN_HEAD_DIM]
        qg = jnp.concatenate([q_ref[rows, (g * rep + r) * ATTN_HEAD_DIM:(g * rep + r + 1) * ATTN_HEAD_DIM]
                              for r in range(rep)], axis=0)
        s = lax.dot_general(kg, qg, NT_DIMS, preferred_element_type=F32) + bias_refs[blk][0, g]
        s_ref[blk, g] = s
        row_max[blk, g] = jnp.maximum(jnp.max(s, axis=0, keepdims=True), sink_ref[g])
    for blk, g in units:
        rows = slice(blk * WINDOW, (blk + 1) * WINDOW)
        vg = kv[blk * WINDOW:(blk + 3) * WINDOW, d_kv + g * ATTN_HEAD_DIM:d_kv + (g + 1) * ATTN_HEAD_DIM]
        m = row_max[blk, g]
        p = jnp.exp(s_ref[blk, g] - m)
        denom = jnp.sum(p, axis=0, keepdims=True) + jnp.exp(sink_ref[g] - m)
        o = lax.dot_general(vg, p.astype(BF16), TN_DIMS, preferred_element_type=F32) / denom
        for r in range(0, rep, 2):
            two = jnp.concatenate([o[:, r * WINDOW:(r + 1) * WINDOW], o[:, (r + 1) * WINDOW:(r + 2) * WINDOW]], axis=0)
            lo = (g * rep + r) * ATTN_HEAD_DIM
            o_ref[rows, lo:lo + 2 * ATTN_HEAD_DIM] = two.T.astype(BF16)


def _attention(q, kv, bias, sink, *, batch, seq):
    t, d_attn = q.shape
    per = ATTN_BLOCKS_PER_STEP
    assert per >= 2 and seq % (per * WINDOW) == 0
    ns = seq // (per * WINDOW)
    nb = seq // WINDOW
    rep = ATTN_HEADS // ATTN_KV_HEADS
    edge = lambda f: pl.BlockSpec((WINDOW, kv.shape[1]), f)
    table = lambda f: pl.BlockSpec((1, ATTN_KV_HEADS, 3 * WINDOW, rep * WINDOW), f)
    return pl.pallas_call(
        _attn_body,
        grid=(batch, ns),
        in_specs=[
            pl.BlockSpec((per * WINDOW, d_attn), lambda b, n: (b * ns + n, 0)),
            edge(lambda b, n: (b * nb + jnp.maximum(per * n - 1, 0), 0)),
            pl.BlockSpec((per * WINDOW, kv.shape[1]), lambda b, n: (b * ns + n, 0)),
            edge(lambda b, n: (b * nb + jnp.minimum(per * n + per, nb - 1), 0)),
            table(lambda b, n: (jnp.where(n == 0, 0, 1), 0, 0, 0)),
            pl.BlockSpec((1, ATTN_KV_HEADS, 3 * WINDOW, rep * WINDOW), lambda b, n: (1, 0, 0, 0),
                         pipeline_mode=pl.Buffered(1)),
            table(lambda b, n: (jnp.where(n == ns - 1, 2, 1), 0, 0, 0)),
            _resident((ATTN_KV_HEADS, 1, rep * WINDOW)),
        ],
        out_specs=pl.BlockSpec((per * WINDOW, d_attn), lambda b, n: (b * ns + n, 0)),
        out_shape=jax.ShapeDtypeStruct((t, d_attn), BF16),
        scratch_shapes=[pltpu.VMEM((per, ATTN_KV_HEADS, 3 * WINDOW, rep * WINDOW), F32)],
        compiler_params=_params("parallel", "parallel"),
        name="attention",
    )(q, kv, kv, kv, bias, bias, bias, sink)


def _attn_bias_tables():
    rep = ATTN_HEADS // ATTN_KV_HEADS
    kj = np.arange(3 * WINDOW)[:, None]
    qi = np.arange(WINDOW)[None, :]
    dist = np.abs(kj - WINDOW - qi).astype(np.float32)
    in_window = dist <= WINDOW
    slopes = 2.0 ** (-8.0 * np.arange(1, ATTN_HEADS + 1, dtype=np.float32) / ATTN_HEADS)
    alibi = -slopes.reshape(ATTN_KV_HEADS, rep)[:, None, :, None] * dist[None, :, None, :]
    cases = []
    for has_prev, has_next in ((False, True), (True, True), (True, False)):
        ok = in_window & ((kj >= WINDOW) | has_prev) & ((kj < 2 * WINDOW) | has_next)
        cases.append(np.where(ok[None, :, None, :], alibi, np.float32(MASKED)))
    out = np.stack(cases).astype(np.float32)
    return out.reshape(3, ATTN_KV_HEADS, 3 * WINDOW, rep * WINDOW)


def _out_mlp_body(x_ref, yf_ref, yb_ref, z_ref, ya_ref, nssm_ref, wo_ref, npost_ref, npre2_ref, wu_ref, wd_ref,
                  npost2_ref, o_ref, *, d_ssm):
    y = yf_ref[...].astype(F32) + yb_ref[...].astype(F32)
    ys = _rmsnorm(y * _silu(z_ref[...].astype(F32)), nssm_ref[...]).astype(BF16)
    mix = (jnp.dot(ys, wo_ref[:d_ssm, :], preferred_element_type=F32)
           + jnp.dot(ya_ref[...], wo_ref[d_ssm:, :], preferred_element_type=F32))
    x1 = x_ref[...] + _rmsnorm(mix, npost_ref[...])
    h = _rmsnorm(x1, npre2_ref[...]).astype(BF16)
    u = jnp.maximum(jnp.dot(h, wu_ref[...], preferred_element_type=F32), 0.0)
    f = jnp.dot((u * u).astype(BF16), wd_ref[...], preferred_element_type=F32)
    o_ref[...] = x1 + _rmsnorm(f, npost2_ref[...])


def _out_mlp(x2, yf, yb, z, ya, nssm, wo, npost, npre2, wu, wd, npost2, *, tm):
    t, d = x2.shape
    d_ssm = yf.shape[1]
    row = lambda width: pl.BlockSpec((tm, width), lambda i: (i, 0))
    return pl.pallas_call(
        functools.partial(_out_mlp_body, d_ssm=d_ssm),
        grid=(t // tm,),
        in_specs=[row(d), row(d_ssm), row(d_ssm), row(d_ssm), row(ya.shape[1]),
                  _resident((1, d_ssm)), _resident(wo.shape), _resident((1, d)),
                  _resident((1, d)), _resident(wu.shape), _resident(wd.shape), _resident((1, d))],
        out_specs=row(d),
        out_shape=jax.ShapeDtypeStruct((t, d), F32),
        compiler_params=_params("parallel"),
        name="out_mlp",
    )(x2, yf, yb, z, ya, nssm, wo, npost, npre2, wu, wd, npost2)


def _layer(x2, p, *, batch, seq):
    d = x2.shape[1]
    d_ssm = SSM_HEADS * SSM_HEAD_DIM
    d_xbc = d_ssm + 2 * SSM_GROUPS * SSM_STATE
    d_attn = ATTN_HEADS * ATTN_HEAD_DIM
    d_kv2 = 2 * ATTN_KV_HEADS * ATTN_HEAD_DIM
    n_dt = 2 * SSM_HEADS
    row = lambda v: v.reshape(1, -1).astype(F32)
    per_stream = lambda v: jnp.broadcast_to(v.reshape(n_dt, 1).astype(F32), (n_dt, CHUNK))

    wa = p["w_in"].astype(BF16)
    o_dt = d_ssm + d_xbc
    wb = jnp.concatenate([wa[:, o_dt + n_dt:], wa[:, o_dt:o_dt + n_dt], jnp.zeros((d, LANES - n_dt), BF16)], axis=1)

    z, xc, bt, cc, acol, drow, q, kv = _in_proj(
        x2, row(p["norm_mix_pre"]), wa, wb, p["conv_w"].astype(F32), row(p["conv_b"]),
        per_stream(p["dt_bias"]), per_stream(p["a_log"]),
        seq=seq, tm=512, d_ssm=d_ssm, d_xbc=d_xbc, d_attn=d_attn, d_kv2=d_kv2)
    dskip = jnp.repeat(p["d_skip"].astype(F32), SSM_HEAD_DIM).reshape(1, d_ssm)
    yf, yb = _ssd_scan(xc, bt, cc, acol, drow, dskip, batch=batch, seq=seq)

    rep = ATTN_HEADS // ATTN_KV_HEADS
    sink = jnp.repeat(p["attn_sink"].astype(F32), WINDOW).reshape(ATTN_KV_HEADS, 1, rep * WINDOW)
    ya = _attention(q, kv, jnp.asarray(_attn_bias_tables()), sink, batch=batch, seq=seq)

    return _out_mlp(x2, yf, yb, z, ya, row(p["ssm_norm"]), p["w_out"].astype(BF16), row(p["norm_mix_post"]),
                    row(p["norm_mlp_pre"]), p["w_up"].astype(BF16), p["w_down"].astype(BF16),
                    row(p["norm_mlp_post"]), tm=512)


def kernel(x, norm_mix_pre, w_in, conv_w, conv_b, dt_bias, a_log, d_skip, ssm_norm, attn_sink, w_out,
           norm_mix_post, norm_mlp_pre, w_up, w_down, norm_mlp_post):
    batch, seq, d = x.shape
    names = ("norm_mix_pre", "w_in", "conv_w", "conv_b", "dt_bias", "a_log", "d_skip", "ssm_norm", "attn_sink",
             "w_out", "norm_mix_post", "norm_mlp_pre", "w_up", "w_down", "norm_mlp_post")
    stacked = (norm_mix_pre, w_in, conv_w, conv_b, dt_bias, a_log, d_skip, ssm_norm, attn_sink, w_out,
               norm_mix_post, norm_mlp_pre, w_up, w_down, norm_mlp_post)
    x2 = x.reshape(batch * seq, d)
    for i in range(w_in.shape[0]):
        x2 = _layer(x2, {k: v[i] for k, v in zip(names, stacked)}, batch=batch, seq=seq)
    return x2.reshape(batch, seq, d)
```

```python
import functools

import jax
import jax.numpy as jnp
import numpy as np
from jax import lax
from jax.experimental import pallas as pl
from jax.experimental.pallas import tpu as pltpu

F32 = jnp.float32
BF16 = jnp.bfloat16

EPS = 1e-6
SSM_HEADS = 16
SSM_HEAD_DIM = 64
SSM_GROUPS = 2
SSM_STATE = 128
CONV_WIDTH = 5
CHUNK = 128
ATTN_HEADS = 16
ATTN_KV_HEADS = 4
ATTN_HEAD_DIM = 64
WINDOW = 128
LANES = 128
SUBLANES = 8
MASKED = -1e30
LOG2E = 1.4426950408889634
BF16_HUGE = 3e38
VMEM_LIMIT = 56 * 1024 * 1024

NT_DIMS = (((1,), (1,)), ((), ()))
TN_DIMS = (((0,), (0,)), ((), ()))


def _rmsnorm(x, w):
    return x * lax.rsqrt(jnp.mean(x * x, axis=-1, keepdims=True) + EPS) * w


def _silu(x):
    return 0.5 * x * (1.0 + jnp.tanh(0.5 * x))


def _params(*semantics):
    return pltpu.CompilerParams(dimension_semantics=semantics, vmem_limit_bytes=VMEM_LIMIT)


def _resident(shape):
    return pl.BlockSpec(shape, lambda *_: (0,) * len(shape), pipeline_mode=pl.Buffered(1))


def _in_proj_body(x_ref, xprev_ref, xnext_ref, g_ref, wa_ref, wb_ref, cw_ref, cb_ref, dtb_ref, alog_ref,
                  z_ref, xc_ref, bt_ref, cc_ref, acol_ref, drow_ref, q_ref, kv_ref,
                  ext_ref, *, tm, blocks_per_seq, d_ssm, d_xbc, d_attn, d_kv2):
    i = pl.program_id(0)
    halo = SUBLANES
    gs = SSM_GROUPS * SSM_STATE
    h = _rmsnorm(x_ref[...], g_ref[...]).astype(BF16)
    h_halo = _rmsnorm(jnp.concatenate([xprev_ref[...], xnext_ref[...]], axis=0), g_ref[...]).astype(BF16)
    he = jnp.concatenate([h, h_halo], axis=0)

    o_kv, o_dt = d_attn, d_attn + d_kv2
    first = i % blocks_per_seq == 0
    last = i % blocks_per_seq == blocks_per_seq - 1
    slab = 2 * LANES
    ext_rows = CHUNK + 2 * halo

    def xbc_slab(c0):
        xs = jnp.dot(he, wa_ref[:, d_ssm + c0:d_ssm + c0 + slab], preferred_element_type=F32)
        ext_ref[0:halo, c0:c0 + slab] = jnp.where(first, 0.0, xs[tm:tm + halo])
        ext_ref[halo:halo + tm, c0:c0 + slab] = xs[0:tm]
        ext_ref[halo + tm:2 * halo + tm, c0:c0 + slab] = jnp.where(last, 0.0, xs[tm + halo:])

    spare = slice(tm + 2 * halo, tm + 2 * halo + SUBLANES)
    unresolved_zero = pl.multiple_of(jnp.minimum(i, 0), SUBLANES)

    def out_slab(o_ref, w_ref, w0, c0, scale):
        acc = jnp.dot(h, w_ref[:, w0 + c0:w0 + c0 + slab], preferred_element_type=F32)
        o_ref[:, c0:c0 + slab] = (acc if scale == 1.0 else acc * scale).astype(BF16)
        ext_ref[spare, 0:LANES] = acc[tm - SUBLANES:tm, 0:LANES]

    def conv_strip(k, c0):
        r0 = k * CHUNK
        e = ext_ref[pl.ds(r0 + unresolved_zero, ext_rows), c0:c0 + LANES]
        acc = cb_ref[:, c0:c0 + LANES] + cw_ref[2:3, c0:c0 + LANES] * e[halo:halo + CHUNK]
        for j in (0, 1, 3, 4):
            tap = pltpu.roll(e, (CONV_WIDTH // 2 - j) % ext_rows, 0)[halo:halo + CHUNK]
            acc = acc + cw_ref[j:j + 1, c0:c0 + LANES] * tap
        act = _silu(acc)
        if c0 < d_ssm:
            xc_ref[r0:r0 + CHUNK, c0:c0 + LANES] = act.astype(BF16)
        elif c0 < d_ssm + gs:
            g = (c0 - d_ssm) // SSM_STATE
            bt_ref[(k * SSM_GROUPS + g) * SSM_STATE:(k * SSM_GROUPS + g + 1) * SSM_STATE, :] = act.T.astype(BF16)
        else:
            cc_ref[r0:r0 + CHUNK, c0 - d_ssm - gs:c0 - d_ssm - gs + LANES] = act.astype(BF16)

    lane = lax.broadcasted_iota(jnp.int32, (2 * SSM_HEADS, CHUNK), 1)
    stream = lax.broadcasted_iota(jnp.int32, (2 * SSM_HEADS, CHUNK), 0)

    def decay_terms(dt_raw, k):
        r0 = k * CHUNK
        v = dt_raw[r0:r0 + CHUNK, :].T[0:2 * SSM_HEADS, :] + dtb_ref[...]
        dt = jnp.maximum(v, 0.0) + jnp.log1p(jnp.exp(-jnp.abs(v)))
        da = dt * (-jnp.exp(alog_ref[...]))
        pre = da
        sh = 1
        while sh < CHUNK:
            pre = pre + jnp.where(lane >= sh, pltpu.roll(pre, sh, 1), 0.0)
            sh *= 2
        total = jnp.broadcast_to(pre[:, CHUNK - 1:CHUNK], pre.shape)
        a = jnp.where(stream < SSM_HEADS, pre, total - pre + da)
        a2 = a * LOG2E
        grow = a2 - jnp.log2(dt)
        wrow = jnp.exp(total - a) * dt
        drow_ref[k * 4 * SSM_HEADS:(k + 1) * 4 * SSM_HEADS, :] = jnp.concatenate([grow, wrow], axis=0)
        a_pad = jnp.concatenate([a2, jnp.zeros((CHUNK - 2 * SSM_HEADS, CHUNK), F32)], axis=0)
        acol_ref[r0:r0 + CHUNK, :] = a_pad.T

    n_chunks = tm // CHUNK
    mxu_tasks = ([functools.partial(xbc_slab, c0) for c0 in range(0, d_xbc, slab)]
                 + [functools.partial(out_slab, z_ref, wa_ref, 0, c0, 1.0) for c0 in range(0, d_ssm, slab)]
                 + [functools.partial(out_slab, q_ref, wb_ref, 0, c0, ATTN_HEAD_DIM ** -0.5)
                    for c0 in range(0, d_attn, slab)]
                 + [functools.partial(out_slab, kv_ref, wb_ref, o_kv, c0, 1.0) for c0 in range(0, d_kv2, slab)])
    strips = [(k, c0) for c0 in range(0, d_xbc, LANES) for k in range(n_chunks)]
    per_task = -(-len(strips) // (len(mxu_tasks) - 4))
    dt_raw = jnp.dot(h, wb_ref[:, o_dt:o_dt + LANES], preferred_element_type=F32)
    decay_chunks = list(range(n_chunks))
    ready_cols = 0
    for task in mxu_tasks:
        task()
        ready_cols += slab * (task.func is xbc_slab)
        if decay_chunks:
            decay_terms(dt_raw, decay_chunks.pop(0))
        for _ in range(per_task):
            if strips and strips[0][1] < ready_cols:
                conv_strip(*strips.pop(0))
    for strip in strips:
        conv_strip(*strip)


def _in_proj(x2, gain, wa, wb, conv_w, conv_b, dt_bias, a_log, *, seq, tm, d_ssm, d_xbc, d_attn, d_kv2):
    t, d = x2.shape
    halo = SUBLANES
    nsub = tm // halo
    nhalo = t // halo
    chunks = tm // CHUNK
    gs = SSM_GROUPS * SSM_STATE
    row = lambda rows, width: pl.BlockSpec((rows, width), lambda i: (i, 0))
    out_shape = [
        jax.ShapeDtypeStruct((t, d_ssm), BF16),
        jax.ShapeDtypeStruct((t, d_ssm), BF16),
        jax.ShapeDtypeStruct((t // CHUNK * gs, CHUNK), BF16),
        jax.ShapeDtypeStruct((t, gs), BF16),
        jax.ShapeDtypeStruct((t, LANES), F32),
        jax.ShapeDtypeStruct((t // CHUNK * 4 * SSM_HEADS, CHUNK), F32),
        jax.ShapeDtypeStruct((t, d_attn), BF16),
        jax.ShapeDtypeStruct((t, d_kv2), BF16),
    ]
    return pl.pallas_call(
        functools.partial(_in_proj_body, tm=tm, blocks_per_seq=seq // tm, d_ssm=d_ssm, d_xbc=d_xbc,
                          d_attn=d_attn, d_kv2=d_kv2),
        grid=(t // tm,),
        in_specs=[row(tm, d),
                  pl.BlockSpec((halo, d), lambda i: (jnp.maximum(i * nsub - 1, 0), 0)),
                  pl.BlockSpec((halo, d), lambda i: (jnp.minimum((i + 1) * nsub, nhalo - 1), 0)),
                  _resident((1, d)),
                  pl.BlockSpec((d, d_ssm + d_xbc), lambda i: (0, 0), pipeline_mode=pl.Buffered(1)),
                  _resident(wb.shape), _resident((CONV_WIDTH, d_xbc)), _resident((1, d_xbc)),
                  _resident((2 * SSM_HEADS, CHUNK)), _resident((2 * SSM_HEADS, CHUNK))],
        out_specs=[row(tm, d_ssm), row(tm, d_ssm), row(chunks * gs, CHUNK), row(tm, gs), row(tm, LANES),
                   row(chunks * 4 * SSM_HEADS, CHUNK), row(tm, d_attn), row(tm, d_kv2)],
        out_shape=out_shape,
        scratch_shapes=[pltpu.VMEM((tm + 2 * halo + SUBLANES, d_xbc), F32)],
        compiler_params=_params("parallel"),
        name="in_proj",
    )(x2, x2, x2, gain, wa, wb, conv_w, conv_b, dt_bias, a_log)


SSD_CHUNKS_PER_STEP = 4


def _ssd_scan_body(xf_ref, btf_ref, ccf_ref, acf_ref, drf_ref, xb_ref, btb_ref, ccb_ref, acb_ref, drb_ref,
                   dskip_ref, yf_ref, yb_ref, stf_ref, stb_ref):
    @pl.when(pl.program_id(1) == 0)
    def _():
        stf_ref[...] = jnp.zeros_like(stf_ref)
        stb_ref[...] = jnp.zeros_like(stb_ref)

    gw = (SSM_HEADS // SSM_GROUPS) * SSM_HEAD_DIM
    pairs_per_group = gw // LANES
    row = lax.broadcasted_iota(jnp.int32, (CHUNK, CHUNK), 0)
    lane = lax.broadcasted_iota(jnp.int32, (CHUNK, CHUNK), 1)
    first_head = lane < SSM_HEAD_DIM
    keep_first = jnp.where(first_head, 1.0, 0.0).astype(BF16)
    keep_second = jnp.where(first_head, 0.0, 1.0).astype(BF16)

    gs = SSM_GROUPS * SSM_STATE

    def visit(x_ref, bt_ref, cc_ref, ac_ref, dr_ref, st_ref, y_ref, direction, k):
        off = direction * SSM_HEADS
        inside = row >= lane if direction == 0 else row <= lane
        edge = CHUNK - 1 if direction == 0 else 0
        rows = slice(k * CHUNK, (k + 1) * CHUNK)
        ac = ac_ref[rows, :]
        dr = dr_ref[k * 4 * SSM_HEADS:(k + 1) * 4 * SSM_HEADS, :]
        for g in range(SSM_GROUPS):
            btg = bt_ref[k * gs + g * SSM_STATE:k * gs + (g + 1) * SSM_STATE, :]
            cg = cc_ref[rows, g * SSM_STATE:(g + 1) * SSM_STATE]
            cb = jnp.dot(cg, btg, preferred_element_type=F32)
            cbm = jnp.where(inside, cb, 0.0).astype(BF16)
            btf = btg.astype(F32)
            st = st_ref[:, g * gw:(g + 1) * gw]
            y_off = jnp.dot(cg, st.astype(BF16), preferred_element_type=F32)
            for jj in range(pairs_per_group):
                j = g * pairs_per_group + jj
                sl = slice(j * LANES, (j + 1) * LANES)
                decays, weighted_bt, cols = [], [], []
                for h in (2 * j, 2 * j + 1):
                    col = jnp.broadcast_to(ac[:, off + h:off + h + 1], (CHUNK, CHUNK))
                    grow = dr[off + h:off + h + 1, :]
                    wrow = dr[2 * SSM_HEADS + off + h:2 * SSM_HEADS + off + h + 1, :]
                    decay = jnp.minimum(jnp.exp2(col - grow).astype(BF16), BF16_HUGE)
                    decays.append(decay * cbm)
                    weighted_bt.append((btf * wrow).astype(BF16))
                    cols.append(col)
                lhs = jnp.concatenate([jnp.concatenate(decays, axis=1), jnp.concatenate(weighted_bt, axis=1)], axis=0)
                xp = x_ref[rows, sl]
                rhs = jnp.concatenate([xp * keep_first, xp * keep_second], axis=0)
                out = jnp.dot(lhs, rhs, preferred_element_type=F32)
                ea = jnp.exp2(jnp.where(first_head, cols[0], cols[1]))
                y = out[0:CHUNK] + ea * y_off[:, jj * LANES:(jj + 1) * LANES]
                if direction == 0:
                    y = y + xp.astype(F32) * dskip_ref[:, sl]
                y_ref[rows, sl] = y.astype(BF16)
                st_ref[:, sl] = st[:, jj * LANES:(jj + 1) * LANES] * ea[edge:edge + 1, :] + out[CHUNK:]

    for k in range(SSD_CHUNKS_PER_STEP):
        visit(xf_ref, btf_ref, ccf_ref, acf_ref, drf_ref, stf_ref, yf_ref, 0, k)
        visit(xb_ref, btb_ref, ccb_ref, acb_ref, drb_ref, stb_ref, yb_ref, 1, SSD_CHUNKS_PER_STEP - 1 - k)


def _ssd_scan(xc, bt, cc, acol, drow, dskip, *, batch, seq):
    t, d_ssm = xc.shape
    per = SSD_CHUNKS_PER_STEP
    assert seq % (per * CHUNK) == 0
    ns = seq // (per * CHUNK)
    gs = SSM_GROUPS * SSM_STATE
    fwd = lambda rows, width: pl.BlockSpec((per * rows, width), lambda b, c: (b * ns + c, 0))
    bwd = lambda rows, width: pl.BlockSpec((per * rows, width), lambda b, c: (b * ns + ns - 1 - c, 0))
    operands = lambda spec: [spec(CHUNK, d_ssm), spec(gs, CHUNK), spec(CHUNK, gs), spec(CHUNK, LANES),
                             spec(4 * SSM_HEADS, CHUNK)]
    return pl.pallas_call(
        _ssd_scan_body,
        grid=(batch, ns),
        in_specs=operands(fwd) + operands(bwd) + [_resident((1, d_ssm))],
        out_specs=[fwd(CHUNK, d_ssm), bwd(CHUNK, d_ssm)],
        out_shape=[jax.ShapeDtypeStruct((t, d_ssm), BF16), jax.ShapeDtypeStruct((t, d_ssm), BF16)],
        scratch_shapes=[pltpu.VMEM((SSM_STATE, d_ssm), F32), pltpu.VMEM((SSM_STATE, d_ssm), F32)],
        compiler_params=_params("arbitrary", "arbitrary"),
        name="ssd_scan",
    )(xc, bt, cc, acol, drow, xc, bt, cc, acol, drow, dskip)


ATTN_BLOCKS_PER_STEP = 4


def _attn_body(q_ref, kvp_ref, kvc_ref, kvn_ref, bias_lo_ref, bias_mid_ref, bias_hi_ref, sink_ref, o_ref, s_ref):
    rep = ATTN_HEADS // ATTN_KV_HEADS
    d_kv = ATTN_KV_HEADS * ATTN_HEAD_DIM
    kv = jnp.concatenate([kvp_ref[...], kvc_ref[...], kvn_ref[...]], axis=0)
    bias_refs = (bias_lo_ref,) + (bias_mid_ref,) * (ATTN_BLOCKS_PER_STEP - 2) + (bias_hi_ref,)
    units = [(blk, g) for blk in range(ATTN_BLOCKS_PER_STEP) for g in range(ATTN_KV_HEADS)]
    row_max = {}
    for blk, g in units:
        rows = slice(blk * WINDOW, (blk + 1) * WINDOW)
        kg = kv[blk * WINDOW:(blk + 3) * WINDOW, g * ATTN_HEAD_DIM:(g + 1) * ATTN_HEAD_DIM]
        qg = jnp.concatenate([q_ref[rows, (g * rep + r) * ATTN_HEAD_DIM:(g * rep + r + 1) * ATTN_HEAD_DIM]
                              for r in range(rep)], axis=0)
        s = lax.dot_general(kg, qg, NT_DIMS, preferred_element_type=F32) + bias_refs[blk][0, g]
        s_ref[blk, g] = s
        row_max[blk, g] = jnp.maximum(jnp.max(s, axis=0, keepdims=True), sink_ref[g])
    for blk, g in units:
        rows = slice(blk * WINDOW, (blk + 1) * WINDOW)
        vg = kv[blk * WINDOW:(blk + 3) * WINDOW, d_kv + g * ATTN_HEAD_DIM:d_kv + (g + 1) * ATTN_HEAD_DIM]
        m = row_max[blk, g]
        p = jnp.exp(s_ref[blk, g] - m)
        denom = jnp.sum(p, axis=0, keepdims=True) + jnp.exp(sink_ref[g] - m)
        o = lax.dot_general(vg, p.astype(BF16), TN_DIMS, preferred_element_type=F32) / denom
        for r in range(0, rep, 2):
            two = jnp.concatenate([o[:, r * WINDOW:(r + 1) * WINDOW], o[:, (r + 1) * WINDOW:(r + 2) * WINDOW]], axis=0)
            lo = (g * rep + r) * ATTN_HEAD_DIM
            o_ref[rows, lo:lo + 2 * ATTN_HEAD_DIM] = two.T.astype(BF16)


def _attention(q, kv, bias, sink, *, batch, seq):
    t, d_attn = q.shape
    per = ATTN_BLOCKS_PER_STEP
    assert per >= 2 and seq % (per * WINDOW) == 0
    ns = seq // (per * WINDOW)
    nb = seq // WINDOW
    rep = ATTN_HEADS // ATTN_KV_HEADS
    edge = lambda f: pl.BlockSpec((WINDOW, kv.shape[1]), f)
    table = lambda f: pl.BlockSpec((1, ATTN_KV_HEADS, 3 * WINDOW, rep * WINDOW), f)
    return pl.pallas_call(
        _attn_body,
        grid=(batch, ns),
        in_specs=[
            pl.BlockSpec((per * WINDOW, d_attn), lambda b, n: (b * ns + n, 0)),
            edge(lambda b, n: (b * nb + jnp.maximum(per * n - 1, 0), 0)),
            pl.BlockSpec((per * WINDOW, kv.shape[1]), lambda b, n: (b * ns + n, 0)),
            edge(lambda b, n: (b * nb + jnp.minimum(per * n + per, nb - 1), 0)),
            table(lambda b, n: (jnp.where(n == 0, 0, 1), 0, 0, 0)),
            pl.BlockSpec((1, ATTN_KV_HEADS, 3 * WINDOW, rep * WINDOW), lambda b, n: (1, 0, 0, 0),
                         pipeline_mode=pl.Buffered(1)),
            table(lambda b, n: (jnp.where(n == ns - 1, 2, 1), 0, 0, 0)),
            _resident((ATTN_KV_HEADS, 1, rep * WINDOW)),
        ],
        out_specs=pl.BlockSpec((per * WINDOW, d_attn), lambda b, n: (b * ns + n, 0)),
        out_shape=jax.ShapeDtypeStruct((t, d_attn), BF16),
        scratch_shapes=[pltpu.VMEM((per, ATTN_KV_HEADS, 3 * WINDOW, rep * WINDOW), F32)],
        compiler_params=_params("parallel", "parallel"),
        name="attention",
    )(q, kv, kv, kv, bias, bias, bias, sink)


def _attn_bias_tables():
    rep = ATTN_HEADS // ATTN_KV_HEADS
    kj = np.arange(3 * WINDOW)[:, None]
    qi = np.arange(WINDOW)[None, :]
    dist = np.abs(kj - WINDOW - qi).astype(np.float32)
    in_window = dist <= WINDOW
    slopes = 2.0 ** (-8.0 * np.arange(1, ATTN_HEADS + 1, dtype=np.float32) / ATTN_HEADS)
    alibi = -slopes.reshape(ATTN_KV_HEADS, rep)[:, None, :, None] * dist[None, :, None, :]
    cases = []
    for has_prev, has_next in ((False, True), (True, True), (True, False)):
        ok = in_window & ((kj >= WINDOW) | has_prev) & ((kj < 2 * WINDOW) | has_next)
        cases.append(np.where(ok[None, :, None, :], alibi, np.float32(MASKED)))
    out = np.stack(cases).astype(np.float32)
    return out.reshape(3, ATTN_KV_HEADS, 3 * WINDOW, rep * WINDOW)


def _out_mlp_body(x_ref, yf_ref, yb_ref, z_ref, ya_ref, nssm_ref, wo_ref, npost_ref, npre2_ref, wu_ref, wd_ref,
                  npost2_ref, o_ref, *, d_ssm):
    y = yf_ref[...].astype(F32) + yb_ref[...].astype(F32)
    ys = _rmsnorm(y * _silu(z_ref[...].astype(F32)), nssm_ref[...]).astype(BF16)
    mix = (jnp.dot(ys, wo_ref[:d_ssm, :], preferred_element_type=F32)
           + jnp.dot(ya_ref[...], wo_ref[d_ssm:, :], preferred_element_type=F32))
    x1 = x_ref[...] + _rmsnorm(mix, npost_ref[...])
    h = _rmsnorm(x1, npre2_ref[...]).astype(BF16)
    u = jnp.maximum(jnp.dot(h, wu_ref[...], preferred_element_type=F32), 0.0)
    f = jnp.dot((u * u).astype(BF16), wd_ref[...], preferred_element_type=F32)
    o_ref[...] = x1 + _rmsnorm(f, npost2_ref[...])


def _out_mlp(x2, yf, yb, z, ya, nssm, wo, npost, npre2, wu, wd, npost2, *, tm):
    t, d = x2.shape
    d_ssm = yf.shape[1]
    row = lambda width: pl.BlockSpec((tm, width), lambda i: (i, 0))
    return pl.pallas_call(
        functools.partial(_out_mlp_body, d_ssm=d_ssm),
        grid=(t // tm,),
        in_specs=[row(d), row(d_ssm), row(d_ssm), row(d_ssm), row(ya.shape[1]),
                  _resident((1, d_ssm)), _resident(wo.shape), _resident((1, d)),
                  _resident((1, d)), _resident(wu.shape), _resident(wd.shape), _resident((1, d))],
        out_specs=row(d),
        out_shape=jax.ShapeDtypeStruct((t, d), F32),
        compiler_params=_params("parallel"),
        name="out_mlp",
    )(x2, yf, yb, z, ya, nssm, wo, npost, npre2, wu, wd, npost2)


def _layer(x2, p, *, batch, seq):
    d = x2.shape[1]
    d_ssm = SSM_HEADS * SSM_HEAD_DIM
    d_xbc = d_ssm + 2 * SSM_GROUPS * SSM_STATE
    d_attn = ATTN_HEADS * ATTN_HEAD_DIM
    d_kv2 = 2 * ATTN_KV_HEADS * ATTN_HEAD_DIM
    n_dt = 2 * SSM_HEADS
    row = lambda v: v.reshape(1, -1).astype(F32)
    per_stream = lambda v: jnp.broadcast_to(v.reshape(n_dt, 1).astype(F32), (n_dt, CHUNK))

    wa = p["w_in"].astype(BF16)
    o_dt = d_ssm + d_xbc
    wb = jnp.concatenate([wa[:, o_dt + n_dt:], wa[:, o_dt:o_dt + n_dt], jnp.zeros((d, LANES - n_dt), BF16)], axis=1)

    z, xc, bt, cc, acol, drow, q, kv = _in_proj(
        x2, row(p["norm_mix_pre"]), wa, wb, p["conv_w"].astype(F32), row(p["conv_b"]),
        per_stream(p["dt_bias"]), per_stream(p["a_log"]),
        seq=seq, tm=512, d_ssm=d_ssm, d_xbc=d_xbc, d_attn=d_attn, d_kv2=d_kv2)
    dskip = jnp.repeat(p["d_skip"].astype(F32), SSM_HEAD_DIM).reshape(1, d_ssm)
    yf, yb = _ssd_scan(xc, bt, cc, acol, drow, dskip, batch=batch, seq=seq)

    rep = ATTN_HEADS // ATTN_KV_HEADS
    sink = jnp.repeat(p["attn_sink"].astype(F32), WINDOW).reshape(ATTN_KV_HEADS, 1, rep * WINDOW)
    ya = _attention(q, kv, jnp.asarray(_attn_bias_tables()), sink, batch=batch, seq=seq)

    return _out_mlp(x2, yf, yb, z, ya, row(p["ssm_norm"]), p["w_out"].astype(BF16), row(p["norm_mix_post"]),
                    row(p["norm_mlp_pre"]), p["w_up"].astype(BF16), p["w_down"].astype(BF16),
                    row(p["norm_mlp_post"]), tm=512)


def kernel(x, norm_mix_pre, w_in, conv_w, conv_b, dt_bias, a_log, d_skip, ssm_norm, attn_sink, w_out,
           norm_mix_post, norm_mlp_pre, w_up, w_down, norm_mlp_post):
    batch, seq, d = x.shape
    names = ("norm_mix_pre", "w_in", "conv_w", "conv_b", "dt_bias", "a_log", "d_skip", "ssm_norm", "attn_sink",
             "w_out", "norm_mix_post", "norm_mlp_pre", "w_up", "w_down", "norm_mlp_post")
    stacked = (norm_mix_pre, w_in, conv_w, conv_b, dt_bias, a_log, d_skip, ssm_norm, attn_sink, w_out,
               norm_mix_post, norm_mlp_pre, w_up, w_down, norm_mlp_post)
    x2 = x.reshape(batch * seq, d)
    for i in range(w_in.shape[0]):
        x2 = _layer(x2, {k: v[i] for k, v in zip(names, stacked)}, batch=batch, seq=seq)
    return x2.reshape(batch, seq, d)
```

```python
import functools

import jax
import jax.numpy as jnp
import numpy as np
from jax import lax
from jax.experimental import pallas as pl
from jax.experimental.pallas import tpu as pltpu

F32 = jnp.float32
BF16 = jnp.bfloat16

EPS = 1e-6
SSM_HEADS = 16
SSM_HEAD_DIM = 64
SSM_GROUPS = 2
SSM_STATE = 128
CONV_WIDTH = 5
CHUNK = 128
ATTN_HEADS = 16
ATTN_KV_HEADS = 4
ATTN_HEAD_DIM = 64
WINDOW = 128
LANES = 128
SUBLANES = 8
MASKED = -1e30
LOG2E = 1.4426950408889634
BF16_HUGE = 3e38
VMEM_LIMIT = 56 * 1024 * 1024

NT_DIMS = (((1,), (1,)), ((), ()))
TN_DIMS = (((0,), (0,)), ((), ()))


def _rmsnorm(x, w):
    return x * lax.rsqrt(jnp.mean(x * x, axis=-1, keepdims=True) + EPS) * w


def _silu(x):
    return 0.5 * x * (1.0 + jnp.tanh(0.5 * x))


def _params(*semantics):
    return pltpu.CompilerParams(dimension_semantics=semantics, vmem_limit_bytes=VMEM_LIMIT)


def _resident(shape):
    return pl.BlockSpec(shape, lambda *_: (0,) * len(shape), pipeline_mode=pl.Buffered(1))


def _in_proj_body(x_ref, xprev_ref, xnext_ref, g_ref, wa_ref, wb_ref, cw_ref, cb_ref, dtb_ref, alog_ref,
                  z_ref, xc_ref, bt_ref, cc_ref, acol_ref, drow_ref, q_ref, kv_ref,
                  ext_ref, *, tm, blocks_per_seq, d_ssm, d_xbc, d_attn, d_kv2):
    i = pl.program_id(0)
    halo = SUBLANES
    gs = SSM_GROUPS * SSM_STATE
    h = _rmsnorm(x_ref[...], g_ref[...]).astype(BF16)
    h_halo = _rmsnorm(jnp.concatenate([xprev_ref[...], xnext_ref[...]], axis=0), g_ref[...]).astype(BF16)
    he = jnp.concatenate([h, h_halo], axis=0)

    o_kv, o_dt = d_attn, d_attn + d_kv2
    first = i % blocks_per_seq == 0
    last = i % blocks_per_seq == blocks_per_seq - 1
    slab = 2 * LANES
    ext_rows = CHUNK + 2 * halo

    def xbc_slab(c0):
        xs = jnp.dot(he, wa_ref[:, d_ssm + c0:d_ssm + c0 + slab], preferred_element_type=F32)
        ext_ref[0:halo, c0:c0 + slab] = jnp.where(first, 0.0, xs[tm:tm + halo])
        ext_ref[halo:halo + tm, c0:c0 + slab] = xs[0:tm]
        ext_ref[halo + tm:2 * halo + tm, c0:c0 + slab] = jnp.where(last, 0.0, xs[tm + halo:])

    spare = slice(tm + 2 * halo, tm + 2 * halo + SUBLANES)
    unresolved_zero = pl.multiple_of(jnp.minimum(i, 0), SUBLANES)

    def out_slab(o_ref, w_ref, w0, c0, scale):
        acc = jnp.dot(h, w_ref[:, w0 + c0:w0 + c0 + slab], preferred_element_type=F32)
        o_ref[:, c0:c0 + slab] = (acc if scale == 1.0 else acc * scale).astype(BF16)
        ext_ref[spare, 0:LANES] = acc[tm - SUBLANES:tm, 0:LANES]

    def conv_strip(k, c0):
        r0 = k * CHUNK
        e = ext_ref[pl.ds(r0 + unresolved_zero, ext_rows), c0:c0 + LANES]
        acc = cb_ref[:, c0:c0 + LANES] + cw_ref[2:3, c0:c0 + LANES] * e[halo:halo + CHUNK]
        for j in (0, 1, 3, 4):
            tap = pltpu.roll(e, (CONV_WIDTH // 2 - j) % ext_rows, 0)[halo:halo + CHUNK]
            acc = acc + cw_ref[j:j + 1, c0:c0 + LANES] * tap
        act = _silu(acc)
        if c0 < d_ssm:
            xc_ref[r0:r0 + CHUNK, c0:c0 + LANES] = act.astype(BF16)
        elif c0 < d_ssm + gs:
            g = (c0 - d_ssm) // SSM_STATE
            bt_ref[(k * SSM_GROUPS + g) * SSM_STATE:(k * SSM_GROUPS + g + 1) * SSM_STATE, :] = act.T.astype(BF16)
        else:
            cc_ref[r0:r0 + CHUNK, c0 - d_ssm - gs:c0 - d_ssm - gs + LANES] = act.astype(BF16)

    lane = lax.broadcasted_iota(jnp.int32, (2 * SSM_HEADS, CHUNK), 1)
    stream = lax.broadcasted_iota(jnp.int32, (2 * SSM_HEADS, CHUNK), 0)

    def decay_terms(dt_raw, k):
        r0 = k * CHUNK
        v = dt_raw[r0:r0 + CHUNK, :].T[0:2 * SSM_HEADS, :] + dtb_ref[...]
        dt = jnp.maximum(v, 0.0) + jnp.log1p(jnp.exp(-jnp.abs(v)))
        da = dt * (-jnp.exp(alog_ref[...]))
        pre = da
        sh = 1
        while sh < CHUNK:
            pre = pre + jnp.where(lane >= sh, pltpu.roll(pre, sh, 1), 0.0)
            sh *= 2
        total = jnp.broadcast_to(pre[:, CHUNK - 1:CHUNK], pre.shape)
        a = jnp.where(stream < SSM_HEADS, pre, total - pre + da)
        a2 = a * LOG2E
        grow = a2 - jnp.log2(dt)
        wrow = jnp.exp(total - a) * dt
        drow_ref[k * 4 * SSM_HEADS:(k + 1) * 4 * SSM_HEADS, :] = jnp.concatenate([grow, wrow], axis=0)
        a_pad = jnp.concatenate([a2, jnp.zeros((CHUNK - 2 * SSM_HEADS, CHUNK), F32)], axis=0)
        acol_ref[r0:r0 + CHUNK, :] = a_pad.T

    n_chunks = tm // CHUNK
    mxu_tasks = ([functools.partial(xbc_slab, c0) for c0 in range(0, d_xbc, slab)]
                 + [functools.partial(out_slab, z_ref, wa_ref, 0, c0, 1.0) for c0 in range(0, d_ssm, slab)]
                 + [functools.partial(out_slab, q_ref, wb_ref, 0, c0, ATTN_HEAD_DIM ** -0.5 * LOG2E)
                    for c0 in range(0, d_attn, slab)]
                 + [functools.partial(out_slab, kv_ref, wb_ref, o_kv, c0, 1.0) for c0 in range(0, d_kv2, slab)])
    strips = [(k, c0) for c0 in range(0, d_xbc, LANES) for k in range(n_chunks)]
    per_task = -(-len(strips) // (len(mxu_tasks) - 4))
    dt_raw = jnp.dot(h, wb_ref[:, o_dt:o_dt + LANES], preferred_element_type=F32)
    decay_chunks = list(range(n_chunks))
    ready_cols = 0
    for task in mxu_tasks:
        task()
        ready_cols += slab * (task.func is xbc_slab)
        if decay_chunks:
            decay_terms(dt_raw, decay_chunks.pop(0))
        for _ in range(per_task):
            if strips and strips[0][1] < ready_cols:
                conv_strip(*strips.pop(0))
    for strip in strips:
        conv_strip(*strip)


def _in_proj(x2, gain, wa, wb, conv_w, conv_b, dt_bias, a_log, *, seq, tm, d_ssm, d_xbc, d_attn, d_kv2):
    t, d = x2.shape
    halo = SUBLANES
    nsub = tm // halo
    nhalo = t // halo
    chunks = tm // CHUNK
    gs = SSM_GROUPS * SSM_STATE
    row = lambda rows, width: pl.BlockSpec((rows, width), lambda i: (i, 0))
    out_shape = [
        jax.ShapeDtypeStruct((t, d_ssm), BF16),
        jax.ShapeDtypeStruct((t, d_ssm), BF16),
        jax.ShapeDtypeStruct((t // CHUNK * gs, CHUNK), BF16),
        jax.ShapeDtypeStruct((t, gs), BF16),
        jax.ShapeDtypeStruct((t, LANES), F32),
        jax.ShapeDtypeStruct((t // CHUNK * 4 * SSM_HEADS, CHUNK), F32),
        jax.ShapeDtypeStruct((t, d_attn), BF16),
        jax.ShapeDtypeStruct((t, d_kv2), BF16),
    ]
    return pl.pallas_call(
        functools.partial(_in_proj_body, tm=tm, blocks_per_seq=seq // tm, d_ssm=d_ssm, d_xbc=d_xbc,
                          d_attn=d_attn, d_kv2=d_kv2),
        grid=(t // tm,),
        in_specs=[row(tm, d),
                  pl.BlockSpec((halo, d), lambda i: (jnp.maximum(i * nsub - 1, 0), 0)),
                  pl.BlockSpec((halo, d), lambda i: (jnp.minimum((i + 1) * nsub, nhalo - 1), 0)),
                  _resident((1, d)),
                  pl.BlockSpec((d, d_ssm + d_xbc), lambda i: (0, 0), pipeline_mode=pl.Buffered(1)),
                  _resident(wb.shape), _resident((CONV_WIDTH, d_xbc)), _resident((1, d_xbc)),
                  _resident((2 * SSM_HEADS, CHUNK)), _resident((2 * SSM_HEADS, CHUNK))],
        out_specs=[row(tm, d_ssm), row(tm, d_ssm), row(chunks * gs, CHUNK), row(tm, gs), row(tm, LANES),
                   row(chunks * 4 * SSM_HEADS, CHUNK), row(tm, d_attn), row(tm, d_kv2)],
        out_shape=out_shape,
        scratch_shapes=[pltpu.VMEM((tm + 2 * halo + SUBLANES, d_xbc), F32)],
        compiler_params=_params("parallel"),
        name="in_proj",
    )(x2, x2, x2, gain, wa, wb, conv_w, conv_b, dt_bias, a_log)


SSD_CHUNKS_PER_STEP = 8


def _ssd_scan_body(xf_ref, btf_ref, ccf_ref, acf_ref, drf_ref, xb_ref, btb_ref, ccb_ref, acb_ref, drb_ref,
                   dskip_ref, yf_ref, yb_ref, stf_ref, stb_ref):
    @pl.when(pl.program_id(1) == 0)
    def _():
        stf_ref[...] = jnp.zeros_like(stf_ref)
        stb_ref[...] = jnp.zeros_like(stb_ref)

    gw = (SSM_HEADS // SSM_GROUPS) * SSM_HEAD_DIM
    pairs_per_group = gw // LANES
    row = lax.broadcasted_iota(jnp.int32, (CHUNK, CHUNK), 0)
    lane = lax.broadcasted_iota(jnp.int32, (CHUNK, CHUNK), 1)
    first_head = lane < SSM_HEAD_DIM
    keep_first = jnp.where(first_head, 1.0, 0.0).astype(BF16)
    keep_second = jnp.where(first_head, 0.0, 1.0).astype(BF16)

    gs = SSM_GROUPS * SSM_STATE

    def visit(x_ref, bt_ref, cc_ref, ac_ref, dr_ref, st_ref, y_ref, direction, k):
        off = direction * SSM_HEADS
        inside = row >= lane if direction == 0 else row <= lane
        edge = CHUNK - 1 if direction == 0 else 0
        rows = slice(k * CHUNK, (k + 1) * CHUNK)
        ac = ac_ref[rows, :]
        dr = dr_ref[k * 4 * SSM_HEADS:(k + 1) * 4 * SSM_HEADS, :]
        for g in range(SSM_GROUPS):
            btg = bt_ref[k * gs + g * SSM_STATE:k * gs + (g + 1) * SSM_STATE, :]
            cg = cc_ref[rows, g * SSM_STATE:(g + 1) * SSM_STATE]
            cb = jnp.dot(cg, btg, preferred_element_type=F32)
            cbm = jnp.where(inside, cb, 0.0).astype(BF16)
            btf = btg.astype(F32)
            st = st_ref[:, g * gw:(g + 1) * gw]
            y_off = jnp.dot(cg, st.astype(BF16), preferred_element_type=F32)
            for jj in range(pairs_per_group):
                j = g * pairs_per_group + jj
                sl = slice(j * LANES, (j + 1) * LANES)
                decays, weighted_bt, cols = [], [], []
                for h in (2 * j, 2 * j + 1):
                    col = jnp.broadcast_to(ac[:, off + h:off + h + 1], (CHUNK, CHUNK))
                    grow = dr[off + h:off + h + 1, :]
                    wrow = dr[2 * SSM_HEADS + off + h:2 * SSM_HEADS + off + h + 1, :]
                    decay = jnp.minimum(jnp.exp2(col - grow).astype(BF16), BF16_HUGE)
                    decays.append(decay * cbm)
                    weighted_bt.append((btf * wrow).astype(BF16))
                    cols.append(col)
                lhs = jnp.concatenate([jnp.concatenate(decays, axis=1), jnp.concatenate(weighted_bt, axis=1)], axis=0)
                xp = x_ref[rows, sl]
                rhs = jnp.concatenate([xp * keep_first, xp * keep_second], axis=0)
                out = jnp.dot(lhs, rhs, preferred_element_type=F32)
                ea = jnp.exp2(jnp.where(first_head, cols[0], cols[1]))
                y = out[0:CHUNK] + ea * y_off[:, jj * LANES:(jj + 1) * LANES]
                if direction == 0:
                    y = y + xp.astype(F32) * dskip_ref[:, sl]
                y_ref[rows, sl] = y.astype(BF16)
                st_ref[:, sl] = st[:, jj * LANES:(jj + 1) * LANES] * ea[edge:edge + 1, :] + out[CHUNK:]

    for k in range(SSD_CHUNKS_PER_STEP):
        visit(xf_ref, btf_ref, ccf_ref, acf_ref, drf_ref, stf_ref, yf_ref, 0, k)
        visit(xb_ref, btb_ref, ccb_ref, acb_ref, drb_ref, stb_ref, yb_ref, 1, SSD_CHUNKS_PER_STEP - 1 - k)


def _ssd_scan(xc, bt, cc, acol, drow, dskip, *, batch, seq):
    t, d_ssm = xc.shape
    per = SSD_CHUNKS_PER_STEP
    assert seq % (per * CHUNK) == 0
    ns = seq // (per * CHUNK)
    gs = SSM_GROUPS * SSM_STATE
    fwd = lambda rows, width: pl.BlockSpec((per * rows, width), lambda b, c: (b * ns + c, 0))
    bwd = lambda rows, width: pl.BlockSpec((per * rows, width), lambda b, c: (b * ns + ns - 1 - c, 0))
    operands = lambda spec: [spec(CHUNK, d_ssm), spec(gs, CHUNK), spec(CHUNK, gs), spec(CHUNK, LANES),
                             spec(4 * SSM_HEADS, CHUNK)]
    return pl.pallas_call(
        _ssd_scan_body,
        grid=(batch, ns),
        in_specs=operands(fwd) + operands(bwd) + [_resident((1, d_ssm))],
        out_specs=[fwd(CHUNK, d_ssm), bwd(CHUNK, d_ssm)],
        out_shape=[jax.ShapeDtypeStruct((t, d_ssm), BF16), jax.ShapeDtypeStruct((t, d_ssm), BF16)],
        scratch_shapes=[pltpu.VMEM((SSM_STATE, d_ssm), F32), pltpu.VMEM((SSM_STATE, d_ssm), F32)],
        compiler_params=_params("arbitrary", "arbitrary"),
        name="ssd_scan",
    )(xc, bt, cc, acol, drow, xc, bt, cc, acol, drow, dskip)


ATTN_BLOCKS_PER_STEP = 4


def _attn_body(q_ref, kvp_ref, kvc_ref, kvn_ref, bias_lo_ref, bias_mid_ref, bias_hi_ref, sink_ref, o_ref, s_ref):
    rep = ATTN_HEADS // ATTN_KV_HEADS
    d_kv = ATTN_KV_HEADS * ATTN_HEAD_DIM
    kv = jnp.concatenate([kvp_ref[...], kvc_ref[...], kvn_ref[...]], axis=0)
    bias_refs = (bias_lo_ref,) + (bias_mid_ref,) * (ATTN_BLOCKS_PER_STEP - 2) + (bias_hi_ref,)
    units = [(blk, g) for blk in range(ATTN_BLOCKS_PER_STEP) for g in range(ATTN_KV_HEADS)]
    row_max = {}
    for blk, g in units:
        rows = slice(blk * WINDOW, (blk + 1) * WINDOW)
        kg = kv[blk * WINDOW:(blk + 3) * WINDOW, g * ATTN_HEAD_DIM:(g + 1) * ATTN_HEAD_DIM]
        qg = jnp.concatenate([q_ref[rows, (g * rep + r) * ATTN_HEAD_DIM:(g * rep + r + 1) * ATTN_HEAD_DIM]
                              for r in range(rep)], axis=0)
        s = lax.dot_general(kg, qg, NT_DIMS, preferred_element_type=F32) + bias_refs[blk][0, g]
        s_ref[blk, g] = s
        row_max[blk, g] = jnp.maximum(jnp.max(s, axis=0, keepdims=True), sink_ref[g])
    for blk, g in units:
        rows = slice(blk * WINDOW, (blk + 1) * WINDOW)
        vg = kv[blk * WINDOW:(blk + 3) * WINDOW, d_kv + g * ATTN_HEAD_DIM:d_kv + (g + 1) * ATTN_HEAD_DIM]
        m = row_max[blk, g]
        p = jnp.exp2(s_ref[blk, g] - m)
        denom = jnp.sum(p, axis=0, keepdims=True) + jnp.exp2(sink_ref[g] - m)
        o = lax.dot_general(vg, p.astype(BF16), TN_DIMS, preferred_element_type=F32) / denom
        for r in range(0, rep, 2):
            two = jnp.concatenate([o[:, r * WINDOW:(r + 1) * WINDOW], o[:, (r + 1) * WINDOW:(r + 2) * WINDOW]], axis=0)
            lo = (g * rep + r) * ATTN_HEAD_DIM
            o_ref[rows, lo:lo + 2 * ATTN_HEAD_DIM] = two.T.astype(BF16)


def _attention(q, kv, bias, sink, *, batch, seq):
    t, d_attn = q.shape
    per = ATTN_BLOCKS_PER_STEP
    assert per >= 2 and seq % (per * WINDOW) == 0
    ns = seq // (per * WINDOW)
    nb = seq // WINDOW
    rep = ATTN_HEADS // ATTN_KV_HEADS
    edge = lambda f: pl.BlockSpec((WINDOW, kv.shape[1]), f)
    table = lambda f: pl.BlockSpec((1, ATTN_KV_HEADS, 3 * WINDOW, rep * WINDOW), f)
    return pl.pallas_call(
        _attn_body,
        grid=(batch, ns),
        in_specs=[
            pl.BlockSpec((per * WINDOW, d_attn), lambda b, n: (b * ns + n, 0)),
            edge(lambda b, n: (b * nb + jnp.maximum(per * n - 1, 0), 0)),
            pl.BlockSpec((per * WINDOW, kv.shape[1]), lambda b, n: (b * ns + n, 0)),
            edge(lambda b, n: (b * nb + jnp.minimum(per * n + per, nb - 1), 0)),
            table(lambda b, n: (jnp.where(n == 0, 0, 1), 0, 0, 0)),
            pl.BlockSpec((1, ATTN_KV_HEADS, 3 * WINDOW, rep * WINDOW), lambda b, n: (1, 0, 0, 0),
                         pipeline_mode=pl.Buffered(1)),
            table(lambda b, n: (jnp.where(n == ns - 1, 2, 1), 0, 0, 0)),
            _resident((ATTN_KV_HEADS, 1, rep * WINDOW)),
        ],
        out_specs=pl.BlockSpec((per * WINDOW, d_attn), lambda b, n: (b * ns + n, 0)),
        out_shape=jax.ShapeDtypeStruct((t, d_attn), BF16),
        scratch_shapes=[pltpu.VMEM((per, ATTN_KV_HEADS, 3 * WINDOW, rep * WINDOW), F32)],
        compiler_params=_params("parallel", "parallel"),
        name="attention",
    )(q, kv, kv, kv, bias, bias, bias, sink)


def _attn_bias_tables():
    rep = ATTN_HEADS // ATTN_KV_HEADS
    kj = np.arange(3 * WINDOW)[:, None]
    qi = np.arange(WINDOW)[None, :]
    dist = np.abs(kj - WINDOW - qi).astype(np.float32)
    in_window = dist <= WINDOW
    slopes = 2.0 ** (-8.0 * np.arange(1, ATTN_HEADS + 1, dtype=np.float32) / ATTN_HEADS)
    alibi = -slopes.reshape(ATTN_KV_HEADS, rep)[:, None, :, None] * dist[None, :, None, :]
    alibi = alibi * np.float32(LOG2E)
    cases = []
    for has_prev, has_next in ((False, True), (True, True), (True, False)):
        ok = in_window & ((kj >= WINDOW) | has_prev) & ((kj < 2 * WINDOW) | has_next)
        cases.append(np.where(ok[None, :, None, :], alibi, np.float32(MASKED)))
    out = np.stack(cases).astype(np.float32)
    return out.reshape(3, ATTN_KV_HEADS, 3 * WINDOW, rep * WINDOW)


def _out_mlp_body(x_ref, yf_ref, yb_ref, z_ref, ya_ref, nssm_ref, wo_ref, npost_ref, npre2_ref, wu_ref, wd_ref,
                  npost2_ref, o_ref, *, d_ssm):
    y = yf_ref[...].astype(F32) + yb_ref[...].astype(F32)
    ys = _rmsnorm(y * _silu(z_ref[...].astype(F32)), nssm_ref[...]).astype(BF16)
    mix = (jnp.dot(ys, wo_ref[:d_ssm, :], preferred_element_type=F32)
           + jnp.dot(ya_ref[...], wo_ref[d_ssm:, :], preferred_element_type=F32))
    x1 = x_ref[...] + _rmsnorm(mix, npost_ref[...])
    h = _rmsnorm(x1, npre2_ref[...]).astype(BF16)
    u = jnp.maximum(jnp.dot(h, wu_ref[...], preferred_element_type=F32), 0.0)
    f = jnp.dot((u * u).astype(BF16), wd_ref[...], preferred_element_type=F32)
    o_ref[...] = x1 + _rmsnorm(f, npost2_ref[...])


def _out_mlp(x2, yf, yb, z, ya, nssm, wo, npost, npre2, wu, wd, npost2, *, tm):
    t, d = x2.shape
    d_ssm = yf.shape[1]
    row = lambda width: pl.BlockSpec((tm, width), lambda i: (i, 0))
    return pl.pallas_call(
        functools.partial(_out_mlp_body, d_ssm=d_ssm),
        grid=(t // tm,),
        in_specs=[row(d), row(d_ssm), row(d_ssm), row(d_ssm), row(ya.shape[1]),
                  _resident((1, d_ssm)), _resident(wo.shape), _resident((1, d)),
                  _resident((1, d)), _resident(wu.shape), _resident(wd.shape), _resident((1, d))],
        out_specs=row(d),
        out_shape=jax.ShapeDtypeStruct((t, d), F32),
        compiler_params=_params("parallel"),
        name="out_mlp",
    )(x2, yf, yb, z, ya, nssm, wo, npost, npre2, wu, wd, npost2)


def _layer(x2, p, *, batch, seq):
    d = x2.shape[1]
    d_ssm = SSM_HEADS * SSM_HEAD_DIM
    d_xbc = d_ssm + 2 * SSM_GROUPS * SSM_STATE
    d_attn = ATTN_HEADS * ATTN_HEAD_DIM
    d_kv2 = 2 * ATTN_KV_HEADS * ATTN_HEAD_DIM
    n_dt = 2 * SSM_HEADS
    row = lambda v: v.reshape(1, -1).astype(F32)
    per_stream = lambda v: jnp.broadcast_to(v.reshape(n_dt, 1).astype(F32), (n_dt, CHUNK))

    wa = p["w_in"].astype(BF16)
    o_dt = d_ssm + d_xbc
    wb = jnp.concatenate([wa[:, o_dt + n_dt:], wa[:, o_dt:o_dt + n_dt], jnp.zeros((d, LANES - n_dt), BF16)], axis=1)

    z, xc, bt, cc, acol, drow, q, kv = _in_proj(
        x2, row(p["norm_mix_pre"]), wa, wb, p["conv_w"].astype(F32), row(p["conv_b"]),
        per_stream(p["dt_bias"]), per_stream(p["a_log"]),
        seq=seq, tm=512, d_ssm=d_ssm, d_xbc=d_xbc, d_attn=d_attn, d_kv2=d_kv2)
    dskip = jnp.repeat(p["d_skip"].astype(F32), SSM_HEAD_DIM).reshape(1, d_ssm)
    yf, yb = _ssd_scan(xc, bt, cc, acol, drow, dskip, batch=batch, seq=seq)

    rep = ATTN_HEADS // ATTN_KV_HEADS
    sink = jnp.repeat(p["attn_sink"].astype(F32) * LOG2E, WINDOW).reshape(ATTN_KV_HEADS, 1, rep * WINDOW)
    ya = _attention(q, kv, jnp.asarray(_attn_bias_tables()), sink, batch=batch, seq=seq)

    return _out_mlp(x2, yf, yb, z, ya, row(p["ssm_norm"]), p["w_out"].astype(BF16), row(p["norm_mix_post"]),
                    row(p["norm_mlp_pre"]), p["w_up"].astype(BF16), p["w_down"].astype(BF16),
                    row(p["norm_mlp_post"]), tm=512)


def kernel(x, norm_mix_pre, w_in, conv_w, conv_b, dt_bias, a_log, d_skip, ssm_norm, attn_sink, w_out,
           norm_mix_post, norm_mlp_pre, w_up, w_down, norm_mlp_post):
    batch, seq, d = x.shape
    names = ("norm_mix_pre", "w_in", "conv_w", "conv_b", "dt_bias", "a_log", "d_skip", "ssm_norm", "attn_sink",
             "w_out", "norm_mix_post", "norm_mlp_pre", "w_up", "w_down", "norm_mlp_post")
    stacked = (norm_mix_pre, w_in, conv_w, conv_b, dt_bias, a_log, d_skip, ssm_norm, attn_sink, w_out,
               norm_mix_post, norm_mlp_pre, w_up, w_down, norm_mlp_post)
    x2 = x.reshape(batch * seq, d)
    for i in range(w_in.shape[0]):
        x2 = _layer(x2, {k: v[i] for k, v in zip(names, stacked)}, batch=batch, seq=seq)
    return x2.reshape(batch, seq, d)
```

```python
import functools

import jax
import jax.numpy as jnp
import numpy as np
from jax import lax
from jax.experimental import pallas as pl
from jax.experimental.pallas import tpu as pltpu

F32 = jnp.float32
BF16 = jnp.bfloat16

EPS = 1e-6
SSM_HEADS = 16
SSM_HEAD_DIM = 64
SSM_GROUPS = 2
SSM_STATE = 128
CONV_WIDTH = 5
CHUNK = 128
ATTN_HEADS = 16
ATTN_KV_HEADS = 4
ATTN_HEAD_DIM = 64
WINDOW = 128
LANES = 128
SUBLANES = 8
MASKED = -1e30
LOG2E = 1.4426950408889634
BF16_HUGE = 3e38
VMEM_LIMIT = 56 * 1024 * 1024

NT_DIMS = (((1,), (1,)), ((), ()))
TN_DIMS = (((0,), (0,)), ((), ()))


def _rmsnorm(x, w):
    return x * lax.rsqrt(jnp.mean(x * x, axis=-1, keepdims=True) + EPS) * w


def _silu(x):
    return 0.5 * x * (1.0 + jnp.tanh(0.5 * x))


def _params(*semantics):
    return pltpu.CompilerParams(dimension_semantics=semantics, vmem_limit_bytes=VMEM_LIMIT)


def _resident(shape):
    return pl.BlockSpec(shape, lambda *_: (0,) * len(shape), pipeline_mode=pl.Buffered(1))


def _in_proj_body(x_ref, xprev_ref, xnext_ref, g_ref, wa_ref, wb_ref, cw_ref, cb_ref, dtb_ref, alog_ref,
                  z_ref, xc_ref, bt_ref, cc_ref, acol_ref, drow_ref, q_ref, kv_ref,
                  ext_ref, *, tm, blocks_per_seq, d_ssm, d_xbc, d_attn, d_kv2):
    i = pl.program_id(0)
    halo = SUBLANES
    gs = SSM_GROUPS * SSM_STATE
    h = _rmsnorm(x_ref[...], g_ref[...]).astype(BF16)
    h_halo = _rmsnorm(jnp.concatenate([xprev_ref[...], xnext_ref[...]], axis=0), g_ref[...]).astype(BF16)
    he = jnp.concatenate([h, h_halo], axis=0)

    o_kv, o_dt = d_attn, d_attn + d_kv2
    first = i % blocks_per_seq == 0
    last = i % blocks_per_seq == blocks_per_seq - 1
    slab = 2 * LANES
    ext_rows = CHUNK + 2 * halo

    def xbc_slab(c0):
        xs = jnp.dot(he, wa_ref[:, d_ssm + c0:d_ssm + c0 + slab], preferred_element_type=F32)
        ext_ref[0:halo, c0:c0 + slab] = jnp.where(first, 0.0, xs[tm:tm + halo])
        ext_ref[halo:halo + tm, c0:c0 + slab] = xs[0:tm]
        ext_ref[halo + tm:2 * halo + tm, c0:c0 + slab] = jnp.where(last, 0.0, xs[tm + halo:])

    spare = slice(tm + 2 * halo, tm + 2 * halo + SUBLANES)
    unresolved_zero = pl.multiple_of(jnp.minimum(i, 0), SUBLANES)

    def out_slab(o_ref, w_ref, w0, c0, scale):
        acc = jnp.dot(h, w_ref[:, w0 + c0:w0 + c0 + slab], preferred_element_type=F32)
        o_ref[:, c0:c0 + slab] = (acc if scale == 1.0 else acc * scale).astype(BF16)
        ext_ref[spare, 0:LANES] = acc[tm - SUBLANES:tm, 0:LANES]

    def conv_strip(k, c0):
        r0 = k * CHUNK
        e = ext_ref[pl.ds(r0 + unresolved_zero, ext_rows), c0:c0 + LANES]
        acc = cb_ref[:, c0:c0 + LANES] + cw_ref[2:3, c0:c0 + LANES] * e[halo:halo + CHUNK]
        for j in (0, 1, 3, 4):
            tap = pltpu.roll(e, (CONV_WIDTH // 2 - j) % ext_rows, 0)[halo:halo + CHUNK]
            acc = acc + cw_ref[j:j + 1, c0:c0 + LANES] * tap
        act = _silu(acc)
        if c0 < d_ssm:
            xc_ref[r0:r0 + CHUNK, c0:c0 + LANES] = act.astype(BF16)
        elif c0 < d_ssm + gs:
            g = (c0 - d_ssm) // SSM_STATE
            bt_ref[(k * SSM_GROUPS + g) * SSM_STATE:(k * SSM_GROUPS + g + 1) * SSM_STATE, :] = act.T.astype(BF16)
        else:
            cc_ref[r0:r0 + CHUNK, c0 - d_ssm - gs:c0 - d_ssm - gs + LANES] = act.astype(BF16)

    lane = lax.broadcasted_iota(jnp.int32, (2 * SSM_HEADS, CHUNK), 1)
    stream = lax.broadcasted_iota(jnp.int32, (2 * SSM_HEADS, CHUNK), 0)

    def decay_terms(dt_raw, k):
        r0 = k * CHUNK
        v = dt_raw[r0:r0 + CHUNK, :].T[0:2 * SSM_HEADS, :] + dtb_ref[...]
        dt = jnp.maximum(v, 0.0) + jnp.log1p(jnp.exp(-jnp.abs(v)))
        da = dt * (-jnp.exp(alog_ref[...]))
        pre = da
        sh = 1
        while sh < CHUNK:
            pre = pre + jnp.where(lane >= sh, pltpu.roll(pre, sh, 1), 0.0)
            sh *= 2
        total = jnp.broadcast_to(pre[:, CHUNK - 1:CHUNK], pre.shape)
        a = jnp.where(stream < SSM_HEADS, pre, total - pre + da)
        a2 = a * LOG2E
        grow = a2 - jnp.log2(dt)
        wrow = jnp.exp(total - a) * dt
        drow_ref[k * 4 * SSM_HEADS:(k + 1) * 4 * SSM_HEADS, :] = jnp.concatenate([grow, wrow], axis=0)
        a_pad = jnp.concatenate([a2, jnp.zeros((CHUNK - 2 * SSM_HEADS, CHUNK), F32)], axis=0)
        acol_ref[r0:r0 + CHUNK, :] = a_pad.T

    n_chunks = tm // CHUNK
    mxu_tasks = ([functools.partial(xbc_slab, c0) for c0 in range(0, d_xbc, slab)]
                 + [functools.partial(out_slab, z_ref, wa_ref, 0, c0, 1.0) for c0 in range(0, d_ssm, slab)]
                 + [functools.partial(out_slab, q_ref, wb_ref, 0, c0, ATTN_HEAD_DIM ** -0.5 * LOG2E)
                    for c0 in range(0, d_attn, slab)]
                 + [functools.partial(out_slab, kv_ref, wb_ref, o_kv, c0, 1.0) for c0 in range(0, d_kv2, slab)])
    strips = [(k, c0) for c0 in range(0, d_xbc, LANES) for k in range(n_chunks)]
    per_task = -(-len(strips) // (len(mxu_tasks) - 4))
    dt_raw = jnp.dot(h, wb_ref[:, o_dt:o_dt + LANES], preferred_element_type=F32)
    decay_chunks = list(range(n_chunks))
    ready_cols = 0
    for task in mxu_tasks:
        task()
        ready_cols += slab * (task.func is xbc_slab)
        if decay_chunks:
            decay_terms(dt_raw, decay_chunks.pop(0))
        for _ in range(per_task):
            if strips and strips[0][1] < ready_cols:
                conv_strip(*strips.pop(0))
    for strip in strips:
        conv_strip(*strip)


def _in_proj(x2, gain, wa, wb, conv_w, conv_b, dt_bias, a_log, *, seq, tm, d_ssm, d_xbc, d_attn, d_kv2):
    t, d = x2.shape
    halo = SUBLANES
    nsub = tm // halo
    nhalo = t // halo
    chunks = tm // CHUNK
    gs = SSM_GROUPS * SSM_STATE
    row = lambda rows, width: pl.BlockSpec((rows, width), lambda i: (i, 0))
    out_shape = [
        jax.ShapeDtypeStruct((t, d_ssm), BF16),
        jax.ShapeDtypeStruct((t, d_ssm), BF16),
        jax.ShapeDtypeStruct((t // CHUNK * gs, CHUNK), BF16),
        jax.ShapeDtypeStruct((t, gs), BF16),
        jax.ShapeDtypeStruct((t, LANES), F32),
        jax.ShapeDtypeStruct((t // CHUNK * 4 * SSM_HEADS, CHUNK), F32),
        jax.ShapeDtypeStruct((t, d_attn), BF16),
        jax.ShapeDtypeStruct((t, d_kv2), BF16),
    ]
    return pl.pallas_call(
        functools.partial(_in_proj_body, tm=tm, blocks_per_seq=seq // tm, d_ssm=d_ssm, d_xbc=d_xbc,
                          d_attn=d_attn, d_kv2=d_kv2),
        grid=(t // tm,),
        in_specs=[row(tm, d),
                  pl.BlockSpec((halo, d), lambda i: (jnp.maximum(i * nsub - 1, 0), 0)),
                  pl.BlockSpec((halo, d), lambda i: (jnp.minimum((i + 1) * nsub, nhalo - 1), 0)),
                  _resident((1, d)),
                  pl.BlockSpec((d, d_ssm + d_xbc), lambda i: (0, 0), pipeline_mode=pl.Buffered(1)),
                  _resident(wb.shape), _resident((CONV_WIDTH, d_xbc)), _resident((1, d_xbc)),
                  _resident((2 * SSM_HEADS, CHUNK)), _resident((2 * SSM_HEADS, CHUNK))],
        out_specs=[row(tm, d_ssm), row(tm, d_ssm), row(chunks * gs, CHUNK), row(tm, gs), row(tm, LANES),
                   row(chunks * 4 * SSM_HEADS, CHUNK), row(tm, d_attn), row(tm, d_kv2)],
        out_shape=out_shape,
        scratch_shapes=[pltpu.VMEM((tm + 2 * halo + SUBLANES, d_xbc), F32)],
        compiler_params=_params("parallel"),
        name="in_proj",
    )(x2, x2, x2, gain, wa, wb, conv_w, conv_b, dt_bias, a_log)


SSD_CHUNKS_PER_STEP = 8


def _ssd_scan_body(xf_ref, btf_ref, ccf_ref, acf_ref, drf_ref, xb_ref, btb_ref, ccb_ref, acb_ref, drb_ref,
                   dskip_ref, yf_ref, yb_ref, stf_ref, stb_ref):
    @pl.when(pl.program_id(1) == 0)
    def _():
        stf_ref[...] = jnp.zeros_like(stf_ref)
        stb_ref[...] = jnp.zeros_like(stb_ref)

    gw = (SSM_HEADS // SSM_GROUPS) * SSM_HEAD_DIM
    pairs_per_group = gw // LANES
    row = lax.broadcasted_iota(jnp.int32, (CHUNK, CHUNK), 0)
    lane = lax.broadcasted_iota(jnp.int32, (CHUNK, CHUNK), 1)
    first_head = lane < SSM_HEAD_DIM
    keep_first = jnp.where(first_head, 1.0, 0.0).astype(BF16)
    keep_second = jnp.where(first_head, 0.0, 1.0).astype(BF16)

    gs = SSM_GROUPS * SSM_STATE

    def visit(x_ref, bt_ref, cc_ref, ac_ref, dr_ref, st_ref, y_ref, direction, k):
        off = direction * SSM_HEADS
        inside = row >= lane if direction == 0 else row <= lane
        edge = CHUNK - 1 if direction == 0 else 0
        rows = slice(k * CHUNK, (k + 1) * CHUNK)
        ac = ac_ref[rows, :]
        dr = dr_ref[k * 4 * SSM_HEADS:(k + 1) * 4 * SSM_HEADS, :]
        for g in range(SSM_GROUPS):
            btg = bt_ref[k * gs + g * SSM_STATE:k * gs + (g + 1) * SSM_STATE, :]
            cg = cc_ref[rows, g * SSM_STATE:(g + 1) * SSM_STATE]
            cb = jnp.dot(cg, btg, preferred_element_type=F32)
            cbm = jnp.where(inside, cb, 0.0).astype(BF16)
            btf = btg.astype(F32)
            st = st_ref[:, g * gw:(g + 1) * gw]
            y_off = jnp.dot(cg, st.astype(BF16), preferred_element_type=F32)
            for jj in range(pairs_per_group):
                j = g * pairs_per_group + jj
                sl = slice(j * LANES, (j + 1) * LANES)
                decays, weighted_bt, cols = [], [], []
                for h in (2 * j, 2 * j + 1):
                    col = jnp.broadcast_to(ac[:, off + h:off + h + 1], (CHUNK, CHUNK))
                    grow = dr[off + h:off + h + 1, :]
                    wrow = dr[2 * SSM_HEADS + off + h:2 * SSM_HEADS + off + h + 1, :]
                    decay = jnp.minimum(jnp.exp2(col - grow).astype(BF16), BF16_HUGE)
                    decays.append(decay * cbm)
                    weighted_bt.append((btf * wrow).astype(BF16))
                    cols.append(col)
                lhs = jnp.concatenate([jnp.concatenate(decays, axis=1), jnp.concatenate(weighted_bt, axis=1)], axis=0)
                xp = x_ref[rows, sl]
                rhs = jnp.concatenate([xp * keep_first, xp * keep_second], axis=0)
                out = jnp.dot(lhs, rhs, preferred_element_type=F32)
                ea = jnp.exp2(jnp.where(first_head, cols[0], cols[1]))
                y = out[0:CHUNK] + ea * y_off[:, jj * LANES:(jj + 1) * LANES]
                if direction == 0:
                    y = y + xp.astype(F32) * dskip_ref[:, sl]
                y_ref[rows, sl] = y.astype(BF16)
                st_ref[:, sl] = st[:, jj * LANES:(jj + 1) * LANES] * ea[edge:edge + 1, :] + out[CHUNK:]

    for k in range(SSD_CHUNKS_PER_STEP):
        visit(xf_ref, btf_ref, ccf_ref, acf_ref, drf_ref, stf_ref, yf_ref, 0, k)
        visit(xb_ref, btb_ref, ccb_ref, acb_ref, drb_ref, stb_ref, yb_ref, 1, SSD_CHUNKS_PER_STEP - 1 - k)


def _ssd_scan(xc, bt, cc, acol, drow, dskip, *, batch, seq):
    t, d_ssm = xc.shape
    per = SSD_CHUNKS_PER_STEP
    assert seq % (per * CHUNK) == 0
    ns = seq // (per * CHUNK)
    gs = SSM_GROUPS * SSM_STATE
    fwd = lambda rows, width: pl.BlockSpec((per * rows, width), lambda b, c: (b * ns + c, 0))
    bwd = lambda rows, width: pl.BlockSpec((per * rows, width), lambda b, c: (b * ns + ns - 1 - c, 0))
    operands = lambda spec: [spec(CHUNK, d_ssm), spec(gs, CHUNK), spec(CHUNK, gs), spec(CHUNK, LANES),
                             spec(4 * SSM_HEADS, CHUNK)]
    return pl.pallas_call(
        _ssd_scan_body,
        grid=(batch, ns),
        in_specs=operands(fwd) + operands(bwd) + [_resident((1, d_ssm))],
        out_specs=[fwd(CHUNK, d_ssm), bwd(CHUNK, d_ssm)],
        out_shape=[jax.ShapeDtypeStruct((t, d_ssm), BF16), jax.ShapeDtypeStruct((t, d_ssm), BF16)],
        scratch_shapes=[pltpu.VMEM((SSM_STATE, d_ssm), F32), pltpu.VMEM((SSM_STATE, d_ssm), F32)],
        compiler_params=_params("arbitrary", "arbitrary"),
        name="ssd_scan",
    )(xc, bt, cc, acol, drow, xc, bt, cc, acol, drow, dskip)


ATTN_BLOCKS_PER_STEP = 4


def _attn_body(q_ref, kvp_ref, kvc_ref, kvn_ref, bias_lo_ref, bias_mid_ref, bias_hi_ref, sink_ref, o_ref, s_ref):
    rep = ATTN_HEADS // ATTN_KV_HEADS
    d_kv = ATTN_KV_HEADS * ATTN_HEAD_DIM
    kv = jnp.concatenate([kvp_ref[...], kvc_ref[...], kvn_ref[...]], axis=0)
    bias_refs = (bias_lo_ref,) + (bias_mid_ref,) * (ATTN_BLOCKS_PER_STEP - 2) + (bias_hi_ref,)
    units = [(blk, g) for blk in range(ATTN_BLOCKS_PER_STEP) for g in range(ATTN_KV_HEADS)]
    row_max = {}
    for blk, g in units:
        rows = slice(blk * WINDOW, (blk + 1) * WINDOW)
        kg = kv[blk * WINDOW:(blk + 3) * WINDOW, g * ATTN_HEAD_DIM:(g + 1) * ATTN_HEAD_DIM]
        qg = jnp.concatenate([q_ref[rows, (g * rep + r) * ATTN_HEAD_DIM:(g * rep + r + 1) * ATTN_HEAD_DIM]
                              for r in range(rep)], axis=0)
        s = lax.dot_general(kg, qg, NT_DIMS, preferred_element_type=F32) + bias_refs[blk][0, g]
        s_ref[blk, g] = s
        row_max[blk, g] = jnp.maximum(jnp.max(s, axis=0, keepdims=True), sink_ref[g])
    for blk, g in units:
        rows = slice(blk * WINDOW, (blk + 1) * WINDOW)
        vg = kv[blk * WINDOW:(blk + 3) * WINDOW, d_kv + g * ATTN_HEAD_DIM:d_kv + (g + 1) * ATTN_HEAD_DIM]
        m = row_max[blk, g]
        p = jnp.exp2(s_ref[blk, g] - m)
        denom = jnp.sum(p, axis=0, keepdims=True) + jnp.exp2(sink_ref[g] - m)
        o = lax.dot_general(vg, p.astype(BF16), TN_DIMS, preferred_element_type=F32) / denom
        for r in range(0, rep, 2):
            two = jnp.concatenate([o[:, r * WINDOW:(r + 1) * WINDOW], o[:, (r + 1) * WINDOW:(r + 2) * WINDOW]], axis=0)
            lo = (g * rep + r) * ATTN_HEAD_DIM
            o_ref[rows, lo:lo + 2 * ATTN_HEAD_DIM] = two.T.astype(BF16)


def _attention(q, kv, bias, sink, *, batch, seq):
    t, d_attn = q.shape
    per = ATTN_BLOCKS_PER_STEP
    assert per >= 2 and seq % (per * WINDOW) == 0
    ns = seq // (per * WINDOW)
    nb = seq // WINDOW
    rep = ATTN_HEADS // ATTN_KV_HEADS
    edge = lambda f: pl.BlockSpec((WINDOW, kv.shape[1]), f)
    table = lambda f: pl.BlockSpec((1, ATTN_KV_HEADS, 3 * WINDOW, rep * WINDOW), f)
    return pl.pallas_call(
        _attn_body,
        grid=(batch, ns),
        in_specs=[
            pl.BlockSpec((per * WINDOW, d_attn), lambda b, n: (b * ns + n, 0)),
            edge(lambda b, n: (b * nb + jnp.maximum(per * n - 1, 0), 0)),
            pl.BlockSpec((per * WINDOW, kv.shape[1]), lambda b, n: (b * ns + n, 0)),
            edge(lambda b, n: (b * nb + jnp.minimum(per * n + per, nb - 1), 0)),
            table(lambda b, n: (jnp.where(n == 0, 0, 1), 0, 0, 0)),
            pl.BlockSpec((1, ATTN_KV_HEADS, 3 * WINDOW, rep * WINDOW), lambda b, n: (1, 0, 0, 0),
                         pipeline_mode=pl.Buffered(1)),
            table(lambda b, n: (jnp.where(n == ns - 1, 2, 1), 0, 0, 0)),
            _resident((ATTN_KV_HEADS, 1, rep * WINDOW)),
        ],
        out_specs=pl.BlockSpec((per * WINDOW, d_attn), lambda b, n: (b * ns + n, 0)),
        out_shape=jax.ShapeDtypeStruct((t, d_attn), BF16),
        scratch_shapes=[pltpu.VMEM((per, ATTN_KV_HEADS, 3 * WINDOW, rep * WINDOW), F32)],
        compiler_params=_params("parallel", "parallel"),
        name="attention",
    )(q, kv, kv, kv, bias, bias, bias, sink)


def _attn_bias_tables():
    rep = ATTN_HEADS // ATTN_KV_HEADS
    kj = np.arange(3 * WINDOW)[:, None]
    qi = np.arange(WINDOW)[None, :]
    dist = np.abs(kj - WINDOW - qi).astype(np.float32)
    in_window = dist <= WINDOW
    slopes = 2.0 ** (-8.0 * np.arange(1, ATTN_HEADS + 1, dtype=np.float32) / ATTN_HEADS)
    alibi = -slopes.reshape(ATTN_KV_HEADS, rep)[:, None, :, None] * dist[None, :, None, :]
    alibi = alibi * np.float32(LOG2E)
    cases = []
    for has_prev, has_next in ((False, True), (True, True), (True, False)):
        ok = in_window & ((kj >= WINDOW) | has_prev) & ((kj < 2 * WINDOW) | has_next)
        cases.append(np.where(ok[None, :, None, :], alibi, np.float32(MASKED)))
    out = np.stack(cases).astype(np.float32)
    return out.reshape(3, ATTN_KV_HEADS, 3 * WINDOW, rep * WINDOW)


def _out_mlp_body(x_ref, yf_ref, yb_ref, z_ref, ya_ref, nssm_ref, wo_ref, npost_ref, npre2_ref, wu_ref, wd_ref,
                  npost2_ref, o_ref, *, d_ssm):
    half = x_ref.shape[0] // 2
    parts = [slice(0, half), slice(half, 2 * half)]
    attn_mix = jnp.dot(ya_ref[...], wo_ref[d_ssm:, :], preferred_element_type=F32)
    ys = []
    for rows in parts:
        y = yf_ref[rows, :].astype(F32) + yb_ref[rows, :].astype(F32)
        ys.append(_rmsnorm(y * _silu(z_ref[rows, :].astype(F32)), nssm_ref[...]).astype(BF16))
    x1, h = [], []
    for rows, ysr in zip(parts, ys):
        mix = jnp.dot(ysr, wo_ref[:d_ssm, :], preferred_element_type=F32) + attn_mix[rows]
        x1.append(x_ref[rows, :] + _rmsnorm(mix, npost_ref[...]))
        h.append(_rmsnorm(x1[-1], npre2_ref[...]).astype(BF16))
    u = [jnp.maximum(jnp.dot(hr, wu_ref[...], preferred_element_type=F32), 0.0) for hr in h]
    f = [jnp.dot((ur * ur).astype(BF16), wd_ref[...], preferred_element_type=F32) for ur in u]
    for rows, x1r, fr in zip(parts, x1, f):
        o_ref[rows, :] = x1r + _rmsnorm(fr, npost2_ref[...])


def _out_mlp(x2, yf, yb, z, ya, nssm, wo, npost, npre2, wu, wd, npost2, *, tm):
    t, d = x2.shape
    d_ssm = yf.shape[1]
    row = lambda width: pl.BlockSpec((tm, width), lambda i: (i, 0))
    return pl.pallas_call(
        functools.partial(_out_mlp_body, d_ssm=d_ssm),
        grid=(t // tm,),
        in_specs=[row(d), row(d_ssm), row(d_ssm), row(d_ssm), row(ya.shape[1]),
                  _resident((1, d_ssm)), _resident(wo.shape), _resident((1, d)),
                  _resident((1, d)), _resident(wu.shape), _resident(wd.shape), _resident((1, d))],
        out_specs=row(d),
        out_shape=jax.ShapeDtypeStruct((t, d), F32),
        compiler_params=_params("parallel"),
        name="out_mlp",
    )(x2, yf, yb, z, ya, nssm, wo, npost, npre2, wu, wd, npost2)


def _layer(x2, p, *, batch, seq):
    d = x2.shape[1]
    d_ssm = SSM_HEADS * SSM_HEAD_DIM
    d_xbc = d_ssm + 2 * SSM_GROUPS * SSM_STATE
    d_attn = ATTN_HEADS * ATTN_HEAD_DIM
    d_kv2 = 2 * ATTN_KV_HEADS * ATTN_HEAD_DIM
    n_dt = 2 * SSM_HEADS
    row = lambda v: v.reshape(1, -1).astype(F32)
    per_stream = lambda v: jnp.broadcast_to(v.reshape(n_dt, 1).astype(F32), (n_dt, CHUNK))

    wa = p["w_in"].astype(BF16)
    o_dt = d_ssm + d_xbc
    wb = jnp.concatenate([wa[:, o_dt + n_dt:], wa[:, o_dt:o_dt + n_dt], jnp.zeros((d, LANES - n_dt), BF16)], axis=1)

    z, xc, bt, cc, acol, drow, q, kv = _in_proj(
        x2, row(p["norm_mix_pre"]), wa, wb, p["conv_w"].astype(F32), row(p["conv_b"]),
        per_stream(p["dt_bias"]), per_stream(p["a_log"]),
        seq=seq, tm=512, d_ssm=d_ssm, d_xbc=d_xbc, d_attn=d_attn, d_kv2=d_kv2)
    dskip = jnp.repeat(p["d_skip"].astype(F32), SSM_HEAD_DIM).reshape(1, d_ssm)
    yf, yb = _ssd_scan(xc, bt, cc, acol, drow, dskip, batch=batch, seq=seq)

    rep = ATTN_HEADS // ATTN_KV_HEADS
    sink = jnp.repeat(p["attn_sink"].astype(F32) * LOG2E, WINDOW).reshape(ATTN_KV_HEADS, 1, rep * WINDOW)
    ya = _attention(q, kv, jnp.asarray(_attn_bias_tables()), sink, batch=batch, seq=seq)

    return _out_mlp(x2, yf, yb, z, ya, row(p["ssm_norm"]), p["w_out"].astype(BF16), row(p["norm_mix_post"]),
                    row(p["norm_mlp_pre"]), p["w_up"].astype(BF16), p["w_down"].astype(BF16),
                    row(p["norm_mlp_post"]), tm=512)


def kernel(x, norm_mix_pre, w_in, conv_w, conv_b, dt_bias, a_log, d_skip, ssm_norm, attn_sink, w_out,
           norm_mix_post, norm_mlp_pre, w_up, w_down, norm_mlp_post):
    batch, seq, d = x.shape
    names = ("norm_mix_pre", "w_in", "conv_w", "conv_b", "dt_bias", "a_log", "d_skip", "ssm_norm", "attn_sink",
             "w_out", "norm_mix_post", "norm_mlp_pre", "w_up", "w_down", "norm_mlp_post")
    stacked = (norm_mix_pre, w_in, conv_w, conv_b, dt_bias, a_log, d_skip, ssm_norm, attn_sink, w_out,
               norm_mix_post, norm_mlp_pre, w_up, w_down, norm_mlp_post)
    x2 = x.reshape(batch * seq, d)
    for i in range(w_in.shape[0]):
        x2 = _layer(x2, {k: v[i] for k, v in zip(names, stacked)}, batch=batch, seq=seq)
    return x2.reshape(batch, seq, d)
```

```python
import functools

import jax
import jax.numpy as jnp
import numpy as np
from jax import lax
from jax.experimental import pallas as pl
from jax.experimental.pallas import tpu as pltpu

F32 = jnp.float32
BF16 = jnp.bfloat16

EPS = 1e-6
SSM_HEADS = 16
SSM_HEAD_DIM = 64
SSM_GROUPS = 2
SSM_STATE = 128
CONV_WIDTH = 5
CHUNK = 128
ATTN_HEADS = 16
ATTN_KV_HEADS = 4
ATTN_HEAD_DIM = 64
WINDOW = 128
LANES = 128
SUBLANES = 8
MASKED = -1e30
LOG2E = 1.4426950408889634
BF16_HUGE = 3e38
VMEM_LIMIT = 56 * 1024 * 1024

NT_DIMS = (((1,), (1,)), ((), ()))
TN_DIMS = (((0,), (0,)), ((), ()))


def _rmsnorm(x, w):
    return x * lax.rsqrt(jnp.mean(x * x, axis=-1, keepdims=True) + EPS) * w


def _silu(x):
    return 0.5 * x * (1.0 + jnp.tanh(0.5 * x))


def _params(*semantics):
    return pltpu.CompilerParams(dimension_semantics=semantics, vmem_limit_bytes=VMEM_LIMIT)


def _resident(shape):
    return pl.BlockSpec(shape, lambda *_: (0,) * len(shape), pipeline_mode=pl.Buffered(1))


def _in_proj_body(x_ref, xprev_ref, xnext_ref, g_ref, w_ref, cw_ref, cb_ref, dtb_ref, alog_ref,
                  z_ref, xc_ref, bt_ref, cc_ref, acol_ref, drow_ref, q_ref, kv_ref,
                  ext_ref, wa_ref, wb_ref, *, tm, blocks_per_seq, d_ssm, d_xbc, d_attn, d_kv2):
    i = pl.program_id(0)
    halo = SUBLANES
    gs = SSM_GROUPS * SSM_STATE

    @pl.when(i == 0)
    def _():
        n_lead, n_dt, wide = d_ssm + d_xbc, 2 * SSM_HEADS, 2 * LANES
        for c0 in range(0, n_lead, wide):
            wa_ref[:, c0:c0 + wide] = w_ref[:, c0:c0 + wide].astype(BF16)
        for c0 in range(0, d_attn + d_kv2, wide):
            wb_ref[:, c0:c0 + wide] = w_ref[:, n_lead + n_dt + c0:n_lead + n_dt + c0 + wide].astype(BF16)
        dt_cols = w_ref[:, n_lead:n_lead + LANES]
        is_dt = lax.broadcasted_iota(jnp.int32, dt_cols.shape, 1) < n_dt
        wb_ref[:, d_attn + d_kv2:] = jnp.where(is_dt, dt_cols, 0.0).astype(BF16)

    h = _rmsnorm(x_ref[...], g_ref[...]).astype(BF16)
    h_halo = _rmsnorm(jnp.concatenate([xprev_ref[...], xnext_ref[...]], axis=0), g_ref[...]).astype(BF16)
    he = jnp.concatenate([h, h_halo], axis=0)

    o_kv, o_dt = d_attn, d_attn + d_kv2
    first = i % blocks_per_seq == 0
    last = i % blocks_per_seq == blocks_per_seq - 1
    slab = 2 * LANES
    ext_rows = CHUNK + 2 * halo

    def xbc_slab(c0):
        xs = jnp.dot(he, wa_ref[:, d_ssm + c0:d_ssm + c0 + slab], preferred_element_type=F32)
        ext_ref[0:halo, c0:c0 + slab] = jnp.where(first, 0.0, xs[tm:tm + halo])
        ext_ref[halo:halo + tm, c0:c0 + slab] = xs[0:tm]
        ext_ref[halo + tm:2 * halo + tm, c0:c0 + slab] = jnp.where(last, 0.0, xs[tm + halo:])

    spare = slice(tm + 2 * halo, tm + 2 * halo + SUBLANES)
    unresolved_zero = pl.multiple_of(jnp.minimum(i, 0), SUBLANES)

    def out_slab(o_ref, w_ref, w0, c0, scale):
        acc = jnp.dot(h, w_ref[:, w0 + c0:w0 + c0 + slab], preferred_element_type=F32)
        o_ref[:, c0:c0 + slab] = (acc if scale == 1.0 else acc * scale).astype(BF16)
        ext_ref[spare, 0:LANES] = acc[tm - SUBLANES:tm, 0:LANES]

    def conv_strip(k, c0):
        r0 = k * CHUNK
        e = ext_ref[pl.ds(r0 + unresolved_zero, ext_rows), c0:c0 + LANES]
        acc = cb_ref[:, c0:c0 + LANES] + cw_ref[2:3, c0:c0 + LANES] * e[halo:halo + CHUNK]
        for j in (0, 1, 3, 4):
            tap = pltpu.roll(e, (CONV_WIDTH // 2 - j) % ext_rows, 0)[halo:halo + CHUNK]
            acc = acc + cw_ref[j:j + 1, c0:c0 + LANES] * tap
        act = _silu(acc)
        if c0 < d_ssm:
            xc_ref[r0:r0 + CHUNK, c0:c0 + LANES] = act.astype(BF16)
        elif c0 < d_ssm + gs:
            g = (c0 - d_ssm) // SSM_STATE
            bt_ref[(k * SSM_GROUPS + g) * SSM_STATE:(k * SSM_GROUPS + g + 1) * SSM_STATE, :] = act.T.astype(BF16)
        else:
            cc_ref[r0:r0 + CHUNK, c0 - d_ssm - gs:c0 - d_ssm - gs + LANES] = act.astype(BF16)

    lane = lax.broadcasted_iota(jnp.int32, (2 * SSM_HEADS, CHUNK), 1)
    stream = lax.broadcasted_iota(jnp.int32, (2 * SSM_HEADS, CHUNK), 0)

    def decay_terms(dt_raw, k):
        r0 = k * CHUNK
        v = dt_raw[r0:r0 + CHUNK, :].T[0:2 * SSM_HEADS, :] + dtb_ref[...]
        dt = jnp.maximum(v, 0.0) + jnp.log1p(jnp.exp(-jnp.abs(v)))
        da = dt * (-jnp.exp(alog_ref[...]))
        pre = da
        sh = 1
        while sh < CHUNK:
            pre = pre + jnp.where(lane >= sh, pltpu.roll(pre, sh, 1), 0.0)
            sh *= 2
        total = jnp.broadcast_to(pre[:, CHUNK - 1:CHUNK], pre.shape)
        a = jnp.where(stream < SSM_HEADS, pre, total - pre + da)
        a2 = a * LOG2E
        grow = a2 - jnp.log2(dt)
        wrow = jnp.exp(total - a) * dt
        drow_ref[k * 4 * SSM_HEADS:(k + 1) * 4 * SSM_HEADS, :] = jnp.concatenate([grow, wrow], axis=0)
        a_pad = jnp.concatenate([a2, jnp.zeros((CHUNK - 2 * SSM_HEADS, CHUNK), F32)], axis=0)
        acol_ref[r0:r0 + CHUNK, :] = a_pad.T

    n_chunks = tm // CHUNK
    mxu_tasks = ([functools.partial(xbc_slab, c0) for c0 in range(0, d_xbc, slab)]
                 + [functools.partial(out_slab, z_ref, wa_ref, 0, c0, 1.0) for c0 in range(0, d_ssm, slab)]
                 + [functools.partial(out_slab, q_ref, wb_ref, 0, c0, ATTN_HEAD_DIM ** -0.5 * LOG2E)
                    for c0 in range(0, d_attn, slab)]
                 + [functools.partial(out_slab, kv_ref, wb_ref, o_kv, c0, 1.0) for c0 in range(0, d_kv2, slab)])
    strips = [(k, c0) for c0 in range(0, d_xbc, LANES) for k in range(n_chunks)]
    per_task = -(-len(strips) // (len(mxu_tasks) - 4))
    dt_raw = jnp.dot(h, wb_ref[:, o_dt:o_dt + LANES], preferred_element_type=F32)
    decay_chunks = list(range(n_chunks))
    ready_cols = 0
    for task in mxu_tasks:
        task()
        ready_cols += slab * (task.func is xbc_slab)
        if decay_chunks:
            decay_terms(dt_raw, decay_chunks.pop(0))
        for _ in range(per_task):
            if strips and strips[0][1] < ready_cols:
                conv_strip(*strips.pop(0))
    for strip in strips:
        conv_strip(*strip)


def _in_proj(x2, gain, w, conv_w, conv_b, dt_bias, a_log, *, seq, tm, d_ssm, d_xbc, d_attn, d_kv2):
    t, d = x2.shape
    halo = SUBLANES
    nsub = tm // halo
    nhalo = t // halo
    chunks = tm // CHUNK
    gs = SSM_GROUPS * SSM_STATE
    row = lambda rows, width: pl.BlockSpec((rows, width), lambda i: (i, 0))
    out_shape = [
        jax.ShapeDtypeStruct((t, d_ssm), BF16),
        jax.ShapeDtypeStruct((t, d_ssm), BF16),
        jax.ShapeDtypeStruct((t // CHUNK * gs, CHUNK), BF16),
        jax.ShapeDtypeStruct((t, gs), BF16),
        jax.ShapeDtypeStruct((t, LANES), F32),
        jax.ShapeDtypeStruct((t // CHUNK * 4 * SSM_HEADS, CHUNK), F32),
        jax.ShapeDtypeStruct((t, d_attn), BF16),
        jax.ShapeDtypeStruct((t, d_kv2), BF16),
    ]
    return pl.pallas_call(
        functools.partial(_in_proj_body, tm=tm, blocks_per_seq=seq // tm, d_ssm=d_ssm, d_xbc=d_xbc,
                          d_attn=d_attn, d_kv2=d_kv2),
        grid=(t // tm,),
        in_specs=[row(tm, d),
                  pl.BlockSpec((halo, d), lambda i: (jnp.maximum(i * nsub - 1, 0), 0)),
                  pl.BlockSpec((halo, d), lambda i: (jnp.minimum((i + 1) * nsub, nhalo - 1), 0)),
                  _resident((1, d)),
                  _resident(w.shape), _resident((CONV_WIDTH, d_xbc)), _resident((1, d_xbc)),
                  _resident((2 * SSM_HEADS, CHUNK)), _resident((2 * SSM_HEADS, CHUNK))],
        out_specs=[row(tm, d_ssm), row(tm, d_ssm), row(chunks * gs, CHUNK), row(tm, gs), row(tm, LANES),
                   row(chunks * 4 * SSM_HEADS, CHUNK), row(tm, d_attn), row(tm, d_kv2)],
        out_shape=out_shape,
        scratch_shapes=[pltpu.VMEM((tm + 2 * halo + SUBLANES, d_xbc), F32),
                        pltpu.VMEM((d, d_ssm + d_xbc), BF16), pltpu.VMEM((d, d_attn + d_kv2 + LANES), BF16)],
        compiler_params=_params("arbitrary"),
        name="in_proj",
    )(x2, x2, x2, gain, w, conv_w, conv_b, dt_bias, a_log)


SSD_CHUNKS_PER_STEP = 8


def _ssd_scan_body(xf_ref, btf_ref, ccf_ref, acf_ref, drf_ref, xb_ref, btb_ref, ccb_ref, acb_ref, drb_ref,
                   dskip_ref, yf_ref, yb_ref, stf_ref, stb_ref):
    @pl.when(pl.program_id(1) == 0)
    def _():
        stf_ref[...] = jnp.zeros_like(stf_ref)
        stb_ref[...] = jnp.zeros_like(stb_ref)

    gw = (SSM_HEADS // SSM_GROUPS) * SSM_HEAD_DIM
    pairs_per_group = gw // LANES
    row = lax.broadcasted_iota(jnp.int32, (CHUNK, CHUNK), 0)
    lane = lax.broadcasted_iota(jnp.int32, (CHUNK, CHUNK), 1)
    first_head = lane < SSM_HEAD_DIM
    keep_first = jnp.where(first_head, 1.0, 0.0).astype(BF16)
    keep_second = jnp.where(first_head, 0.0, 1.0).astype(BF16)

    gs = SSM_GROUPS * SSM_STATE

    def visit(x_ref, bt_ref, cc_ref, ac_ref, dr_ref, st_ref, y_ref, direction, k):
        off = direction * SSM_HEADS
        inside = row >= lane if direction == 0 else row <= lane
        edge = CHUNK - 1 if direction == 0 else 0
        rows = slice(k * CHUNK, (k + 1) * CHUNK)
        ac = ac_ref[rows, :]
        dr = dr_ref[k * 4 * SSM_HEADS:(k + 1) * 4 * SSM_HEADS, :]
        for g in range(SSM_GROUPS):
            btg = bt_ref[k * gs + g * SSM_STATE:k * gs + (g + 1) * SSM_STATE, :]
            cg = cc_ref[rows, g * SSM_STATE:(g + 1) * SSM_STATE]
            cb = jnp.dot(cg, btg, preferred_element_type=F32)
            cbm = jnp.where(inside, cb, 0.0).astype(BF16)
            btf = btg.astype(F32)
            st = st_ref[:, g * gw:(g + 1) * gw]
            y_off = jnp.dot(cg, st.astype(BF16), preferred_element_type=F32)
            for jj in range(pairs_per_group):
                j = g * pairs_per_group + jj
                sl = slice(j * LANES, (j + 1) * LANES)
                decays, weighted_bt, cols = [], [], []
                for h in (2 * j, 2 * j + 1):
                    col = jnp.broadcast_to(ac[:, off + h:off + h + 1], (CHUNK, CHUNK))
                    grow = dr[off + h:off + h + 1, :]
                    wrow = dr[2 * SSM_HEADS + off + h:2 * SSM_HEADS + off + h + 1, :]
                    decay = jnp.minimum(jnp.exp2(col - grow).astype(BF16), BF16_HUGE)
                    decays.append(decay * cbm)
                    weighted_bt.append((btf * wrow).astype(BF16))
                    cols.append(col)
                lhs = jnp.concatenate([jnp.concatenate(decays, axis=1), jnp.concatenate(weighted_bt, axis=1)], axis=0)
                xp = x_ref[rows, sl]
                rhs = jnp.concatenate([xp * keep_first, xp * keep_second], axis=0)
                out = jnp.dot(lhs, rhs, preferred_element_type=F32)
                ea = jnp.exp2(jnp.where(first_head, cols[0], cols[1]))
                y = out[0:CHUNK] + ea * y_off[:, jj * LANES:(jj + 1) * LANES]
                if direction == 0:
                    y = y + xp.astype(F32) * dskip_ref[:, sl]
                y_ref[rows, sl] = y.astype(BF16)
                st_ref[:, sl] = st[:, jj * LANES:(jj + 1) * LANES] * ea[edge:edge + 1, :] + out[CHUNK:]

    for k in range(SSD_CHUNKS_PER_STEP):
        visit(xf_ref, btf_ref, ccf_ref, acf_ref, drf_ref, stf_ref, yf_ref, 0, k)
        visit(xb_ref, btb_ref, ccb_ref, acb_ref, drb_ref, stb_ref, yb_ref, 1, SSD_CHUNKS_PER_STEP - 1 - k)


def _ssd_scan(xc, bt, cc, acol, drow, dskip, *, batch, seq):
    t, d_ssm = xc.shape
    per = SSD_CHUNKS_PER_STEP
    assert seq % (per * CHUNK) == 0
    ns = seq // (per * CHUNK)
    gs = SSM_GROUPS * SSM_STATE
    fwd = lambda rows, width: pl.BlockSpec((per * rows, width), lambda b, c: (b * ns + c, 0))
    bwd = lambda rows, width: pl.BlockSpec((per * rows, width), lambda b, c: (b * ns + ns - 1 - c, 0))
    operands = lambda spec: [spec(CHUNK, d_ssm), spec(gs, CHUNK), spec(CHUNK, gs), spec(CHUNK, LANES),
                             spec(4 * SSM_HEADS, CHUNK)]
    return pl.pallas_call(
        _ssd_scan_body,
        grid=(batch, ns),
        in_specs=operands(fwd) + operands(bwd) + [_resident((1, d_ssm))],
        out_specs=[fwd(CHUNK, d_ssm), bwd(CHUNK, d_ssm)],
        out_shape=[jax.ShapeDtypeStruct((t, d_ssm), BF16), jax.ShapeDtypeStruct((t, d_ssm), BF16)],
        scratch_shapes=[pltpu.VMEM((SSM_STATE, d_ssm), F32), pltpu.VMEM((SSM_STATE, d_ssm), F32)],
        compiler_params=_params("arbitrary", "arbitrary"),
        name="ssd_scan",
    )(xc, bt, cc, acol, drow, xc, bt, cc, acol, drow, dskip)


ATTN_BLOCKS_PER_STEP = 4


def _attn_body(q_ref, kvp_ref, kvc_ref, kvn_ref, bias_lo_ref, bias_mid_ref, bias_hi_ref, sink_ref, o_ref, s_ref):
    rep = ATTN_HEADS // ATTN_KV_HEADS
    d_kv = ATTN_KV_HEADS * ATTN_HEAD_DIM
    kv = jnp.concatenate([kvp_ref[...], kvc_ref[...], kvn_ref[...]], axis=0)
    bias_refs = (bias_lo_ref,) + (bias_mid_ref,) * (ATTN_BLOCKS_PER_STEP - 2) + (bias_hi_ref,)
    units = [(blk, g) for blk in range(ATTN_BLOCKS_PER_STEP) for g in range(ATTN_KV_HEADS)]
    row_max = {}
    for blk, g in units:
        rows = slice(blk * WINDOW, (blk + 1) * WINDOW)
        kg = kv[blk * WINDOW:(blk + 3) * WINDOW, g * ATTN_HEAD_DIM:(g + 1) * ATTN_HEAD_DIM]
        qg = jnp.concatenate([q_ref[rows, (g * rep + r) * ATTN_HEAD_DIM:(g * rep + r + 1) * ATTN_HEAD_DIM]
                              for r in range(rep)], axis=0)
        s = lax.dot_general(kg, qg, NT_DIMS, preferred_element_type=F32) + bias_refs[blk][0, g]
        s_ref[blk, g] = s
        row_max[blk, g] = jnp.maximum(jnp.max(s, axis=0, keepdims=True), sink_ref[g])
    for blk, g in units:
        rows = slice(blk * WINDOW, (blk + 1) * WINDOW)
        vg = kv[blk * WINDOW:(blk + 3) * WINDOW, d_kv + g * ATTN_HEAD_DIM:d_kv + (g + 1) * ATTN_HEAD_DIM]
        m = row_max[blk, g]
        p = jnp.exp2(s_ref[blk, g] - m)
        denom = jnp.sum(p, axis=0, keepdims=True) + jnp.exp2(sink_ref[g] - m)
        o = lax.dot_general(vg, p.astype(BF16), TN_DIMS, preferred_element_type=F32) / denom
        for r in range(0, rep, 2):
            two = jnp.concatenate([o[:, r * WINDOW:(r + 1) * WINDOW], o[:, (r + 1) * WINDOW:(r + 2) * WINDOW]], axis=0)
            lo = (g * rep + r) * ATTN_HEAD_DIM
            o_ref[rows, lo:lo + 2 * ATTN_HEAD_DIM] = two.T.astype(BF16)


def _attention(q, kv, bias, sink, *, batch, seq):
    t, d_attn = q.shape
    per = ATTN_BLOCKS_PER_STEP
    assert per >= 2 and seq % (per * WINDOW) == 0
    ns = seq // (per * WINDOW)
    nb = seq // WINDOW
    rep = ATTN_HEADS // ATTN_KV_HEADS
    edge = lambda f: pl.BlockSpec((WINDOW, kv.shape[1]), f)
    table = lambda f: pl.BlockSpec((1, ATTN_KV_HEADS, 3 * WINDOW, rep * WINDOW), f)
    return pl.pallas_call(
        _attn_body,
        grid=(batch, ns),
        in_specs=[
            pl.BlockSpec((per * WINDOW, d_attn), lambda b, n: (b * ns + n, 0)),
            edge(lambda b, n: (b * nb + jnp.maximum(per * n - 1, 0), 0)),
            pl.BlockSpec((per * WINDOW, kv.shape[1]), lambda b, n: (b * ns + n, 0)),
            edge(lambda b, n: (b * nb + jnp.minimum(per * n + per, nb - 1), 0)),
            table(lambda b, n: (jnp.where(n == 0, 0, 1), 0, 0, 0)),
            pl.BlockSpec((1, ATTN_KV_HEADS, 3 * WINDOW, rep * WINDOW), lambda b, n: (1, 0, 0, 0),
                         pipeline_mode=pl.Buffered(1)),
            table(lambda b, n: (jnp.where(n == ns - 1, 2, 1), 0, 0, 0)),
            _resident((ATTN_KV_HEADS, 1, rep * WINDOW)),
        ],
        out_specs=pl.BlockSpec((per * WINDOW, d_attn), lambda b, n: (b * ns + n, 0)),
        out_shape=jax.ShapeDtypeStruct((t, d_attn), BF16),
        scratch_shapes=[pltpu.VMEM((per, ATTN_KV_HEADS, 3 * WINDOW, rep * WINDOW), F32)],
        compiler_params=_params("parallel", "parallel"),
        name="attention",
    )(q, kv, kv, kv, bias, bias, bias, sink)


def _attn_bias_tables():
    rep = ATTN_HEADS // ATTN_KV_HEADS
    kj = np.arange(3 * WINDOW)[:, None]
    qi = np.arange(WINDOW)[None, :]
    dist = np.abs(kj - WINDOW - qi).astype(np.float32)
    in_window = dist <= WINDOW
    slopes = 2.0 ** (-8.0 * np.arange(1, ATTN_HEADS + 1, dtype=np.float32) / ATTN_HEADS)
    alibi = -slopes.reshape(ATTN_KV_HEADS, rep)[:, None, :, None] * dist[None, :, None, :]
    alibi = alibi * np.float32(LOG2E)
    cases = []
    for has_prev, has_next in ((False, True), (True, True), (True, False)):
        ok = in_window & ((kj >= WINDOW) | has_prev) & ((kj < 2 * WINDOW) | has_next)
        cases.append(np.where(ok[None, :, None, :], alibi, np.float32(MASKED)))
    out = np.stack(cases).astype(np.float32)
    return out.reshape(3, ATTN_KV_HEADS, 3 * WINDOW, rep * WINDOW)


OUT_MLP_PARTS = 2


def _out_mlp_body(x_ref, yf_ref, yb_ref, z_ref, ya_ref, nssm_ref, wo_ref, npost_ref, npre2_ref, wu_ref, wd_ref,
                  npost2_ref, o_ref, *, d_ssm):
    part = x_ref.shape[0] // OUT_MLP_PARTS
    parts = [slice(n * part, (n + 1) * part) for n in range(OUT_MLP_PARTS)]
    attn_mix = jnp.dot(ya_ref[...], wo_ref[d_ssm:, :], preferred_element_type=F32)
    ys = []
    for rows in parts:
        y = yf_ref[rows, :].astype(F32) + yb_ref[rows, :].astype(F32)
        ys.append(_rmsnorm(y * _silu(z_ref[rows, :].astype(F32)), nssm_ref[...]).astype(BF16))
    x1, h = [], []
    for rows, ysr in zip(parts, ys):
        mix = jnp.dot(ysr, wo_ref[:d_ssm, :], preferred_element_type=F32) + attn_mix[rows]
        x1.append(x_ref[rows, :] + _rmsnorm(mix, npost_ref[...]))
        h.append(_rmsnorm(x1[-1], npre2_ref[...]).astype(BF16))
    u = [jnp.maximum(jnp.dot(hr, wu_ref[...], preferred_element_type=F32), 0.0) for hr in h]
    f = [jnp.dot((ur * ur).astype(BF16), wd_ref[...], preferred_element_type=F32) for ur in u]
    for rows, x1r, fr in zip(parts, x1, f):
        o_ref[rows, :] = x1r + _rmsnorm(fr, npost2_ref[...])


def _out_mlp(x2, yf, yb, z, ya, nssm, wo, npost, npre2, wu, wd, npost2, *, tm):
    t, d = x2.shape
    d_ssm = yf.shape[1]
    row = lambda width: pl.BlockSpec((tm, width), lambda i: (i, 0))
    return pl.pallas_call(
        functools.partial(_out_mlp_body, d_ssm=d_ssm),
        grid=(t // tm,),
        in_specs=[row(d), row(d_ssm), row(d_ssm), row(d_ssm), row(ya.shape[1]),
                  _resident((1, d_ssm)), _resident(wo.shape), _resident((1, d)),
                  _resident((1, d)), _resident(wu.shape), _resident(wd.shape), _resident((1, d))],
        out_specs=row(d),
        out_shape=jax.ShapeDtypeStruct((t, d), F32),
        compiler_params=_params("parallel"),
        name="out_mlp",
    )(x2, yf, yb, z, ya, nssm, wo, npost, npre2, wu, wd, npost2)


def _layer(x2, p, *, batch, seq):
    d = x2.shape[1]
    d_ssm = SSM_HEADS * SSM_HEAD_DIM
    d_xbc = d_ssm + 2 * SSM_GROUPS * SSM_STATE
    d_attn = ATTN_HEADS * ATTN_HEAD_DIM
    d_kv2 = 2 * ATTN_KV_HEADS * ATTN_HEAD_DIM
    n_dt = 2 * SSM_HEADS
    row = lambda v: v.reshape(1, -1).astype(F32)
    per_stream = lambda v: jnp.broadcast_to(v.reshape(n_dt, 1).astype(F32), (n_dt, CHUNK))

    z, xc, bt, cc, acol, drow, q, kv = _in_proj(
        x2, row(p["norm_mix_pre"]), p["w_in"].astype(F32), p["conv_w"].astype(F32), row(p["conv_b"]),
        per_stream(p["dt_bias"]), per_stream(p["a_log"]),
        seq=seq, tm=512, d_ssm=d_ssm, d_xbc=d_xbc, d_attn=d_attn, d_kv2=d_kv2)
    dskip = jnp.repeat(p["d_skip"].astype(F32), SSM_HEAD_DIM).reshape(1, d_ssm)
    yf, yb = _ssd_scan(xc, bt, cc, acol, drow, dskip, batch=batch, seq=seq)

    rep = ATTN_HEADS // ATTN_KV_HEADS
    sink = jnp.repeat(p["attn_sink"].astype(F32) * LOG2E, WINDOW).reshape(ATTN_KV_HEADS, 1, rep * WINDOW)
    ya = _attention(q, kv, jnp.asarray(_attn_bias_tables()), sink, batch=batch, seq=seq)

    return _out_mlp(x2, yf, yb, z, ya, row(p["ssm_norm"]), p["w_out"].astype(BF16), row(p["norm_mix_post"]),
                    row(p["norm_mlp_pre"]), p["w_up"].astype(BF16), p["w_down"].astype(BF16),
                    row(p["norm_mlp_post"]), tm=512)


def kernel(x, norm_mix_pre, w_in, conv_w, conv_b, dt_bias, a_log, d_skip, ssm_norm, attn_sink, w_out,
           norm_mix_post, norm_mlp_pre, w_up, w_down, norm_mlp_post):
    batch, seq, d = x.shape
    names = ("norm_mix_pre", "w_in", "conv_w", "conv_b", "dt_bias", "a_log", "d_skip", "ssm_norm", "attn_sink",
             "w_out", "norm_mix_post", "norm_mlp_pre", "w_up", "w_down", "norm_mlp_post")
    stacked = (norm_mix_pre, w_in, conv_w, conv_b, dt_bias, a_log, d_skip, ssm_norm, attn_sink, w_out,
               norm_mix_post, norm_mlp_pre, w_up, w_down, norm_mlp_post)
    x2 = x.reshape(batch * seq, d)
    for i in range(w_in.shape[0]):
        x2 = _layer(x2, {k: v[i] for k, v in zip(names, stacked)}, batch=batch, seq=seq)
    return x2.reshape(batch, seq, d)
```

```python
import functools

import jax
import jax.numpy as jnp
import numpy as np
from jax import lax
from jax.experimental import pallas as pl
from jax.experimental.pallas import tpu as pltpu

F32 = jnp.float32
BF16 = jnp.bfloat16

EPS = 1e-6
SSM_HEADS = 16
SSM_HEAD_DIM = 64
SSM_GROUPS = 2
SSM_STATE = 128
CONV_WIDTH = 5
CHUNK = 128
ATTN_HEADS = 16
ATTN_KV_HEADS = 4
ATTN_HEAD_DIM = 64
WINDOW = 128
LANES = 128
SUBLANES = 8
MASKED = -1e30
LOG2E = 1.4426950408889634
BF16_HUGE = 3e38
VMEM_LIMIT = 56 * 1024 * 1024

NT_DIMS = (((1,), (1,)), ((), ()))
TN_DIMS = (((0,), (0,)), ((), ()))


def _rmsnorm(x, w):
    return x * lax.rsqrt(jnp.mean(x * x, axis=-1, keepdims=True) + EPS) * w


def _silu(x):
    return 0.5 * x * (1.0 + jnp.tanh(0.5 * x))


def _params(*semantics):
    return pltpu.CompilerParams(dimension_semantics=semantics, vmem_limit_bytes=VMEM_LIMIT)


def _resident(shape):
    return pl.BlockSpec(shape, lambda *_: (0,) * len(shape), pipeline_mode=pl.Buffered(1))


def _in_proj_body(x_ref, xprev_ref, xnext_ref, g_ref, wa_ref, wb_ref, cw_ref, cb_ref, dtb_ref, alog_ref,
                  z_ref, xc_ref, bt_ref, cc_ref, acol_ref, drow_ref, q_ref, kv_ref,
                  ext_ref, *, tm, blocks_per_seq, d_ssm, d_xbc, d_attn, d_kv2):
    i = pl.program_id(0)
    halo = SUBLANES
    gs = SSM_GROUPS * SSM_STATE
    h = _rmsnorm(x_ref[...], g_ref[...]).astype(BF16)
    h_halo = _rmsnorm(jnp.concatenate([xprev_ref[...], xnext_ref[...]], axis=0), g_ref[...]).astype(BF16)
    he = jnp.concatenate([h, h_halo], axis=0)

    o_kv, o_dt = d_attn, d_attn + d_kv2
    first = i % blocks_per_seq == 0
    last = i % blocks_per_seq == blocks_per_seq - 1
    slab = 2 * LANES
    ext_rows = CHUNK + 2 * halo

    def xbc_slab(c0):
        xs = jnp.dot(he, wa_ref[:, d_ssm + c0:d_ssm + c0 + slab], preferred_element_type=F32)
        ext_ref[0:halo, c0:c0 + slab] = jnp.where(first, 0.0, xs[tm:tm + halo])
        ext_ref[halo:halo + tm, c0:c0 + slab] = xs[0:tm]
        ext_ref[halo + tm:2 * halo + tm, c0:c0 + slab] = jnp.where(last, 0.0, xs[tm + halo:])

    spare = slice(tm + 2 * halo, tm + 2 * halo + SUBLANES)
    unresolved_zero = pl.multiple_of(jnp.minimum(i, 0), SUBLANES)

    def out_slab(o_ref, w_ref, w0, c0, scale):
        acc = jnp.dot(h, w_ref[:, w0 + c0:w0 + c0 + slab], preferred_element_type=F32)
        o_ref[:, c0:c0 + slab] = (acc if scale == 1.0 else acc * scale).astype(BF16)
        ext_ref[spare, 0:LANES] = acc[tm - SUBLANES:tm, 0:LANES]

    def conv_strip(k, c0):
        r0 = k * CHUNK
        e = ext_ref[pl.ds(r0 + unresolved_zero, ext_rows), c0:c0 + LANES]
        acc = cb_ref[:, c0:c0 + LANES] + cw_ref[2:3, c0:c0 + LANES] * e[halo:halo + CHUNK]
        for j in (0, 1, 3, 4):
            tap = pltpu.roll(e, (CONV_WIDTH // 2 - j) % ext_rows, 0)[halo:halo + CHUNK]
            acc = acc + cw_ref[j:j + 1, c0:c0 + LANES] * tap
        act = _silu(acc)
        if c0 < d_ssm:
            xc_ref[r0:r0 + CHUNK, c0:c0 + LANES] = act.astype(BF16)
        elif c0 < d_ssm + gs:
            g = (c0 - d_ssm) // SSM_STATE
            bt_ref[(k * SSM_GROUPS + g) * SSM_STATE:(k * SSM_GROUPS + g + 1) * SSM_STATE, :] = act.T.astype(BF16)
        else:
            cc_ref[r0:r0 + CHUNK, c0 - d_ssm - gs:c0 - d_ssm - gs + LANES] = act.astype(BF16)

    lane = lax.broadcasted_iota(jnp.int32, (2 * SSM_HEADS, CHUNK), 1)
    stream = lax.broadcasted_iota(jnp.int32, (2 * SSM_HEADS, CHUNK), 0)

    def decay_terms(dt_raw, k):
        r0 = k * CHUNK
        v = dt_raw[r0:r0 + CHUNK, :].T[0:2 * SSM_HEADS, :] + dtb_ref[...]
        dt = jnp.maximum(v, 0.0) + jnp.log1p(jnp.exp(-jnp.abs(v)))
        da = dt * (-jnp.exp(alog_ref[...]))
        pre = da
        sh = 1
        while sh < CHUNK:
            pre = pre + jnp.where(lane >= sh, pltpu.roll(pre, sh, 1), 0.0)
            sh *= 2
        total = jnp.broadcast_to(pre[:, CHUNK - 1:CHUNK], pre.shape)
        a = jnp.where(stream < SSM_HEADS, pre, total - pre + da)
        a2 = a * LOG2E
        grow = a2 - jnp.log2(dt)
        wrow = jnp.exp(total - a) * dt
        drow_ref[k * 4 * SSM_HEADS:(k + 1) * 4 * SSM_HEADS, :] = jnp.concatenate([grow, wrow], axis=0)
        a_pad = jnp.concatenate([a2, jnp.zeros((CHUNK - 2 * SSM_HEADS, CHUNK), F32)], axis=0)
        acol_ref[r0:r0 + CHUNK, :] = a_pad.T

    n_chunks = tm // CHUNK
    mxu_tasks = ([functools.partial(xbc_slab, c0) for c0 in range(0, d_xbc, slab)]
                 + [functools.partial(out_slab, z_ref, wa_ref, 0, c0, 1.0) for c0 in range(0, d_ssm, slab)]
                 + [functools.partial(out_slab, q_ref, wb_ref, 0, c0, ATTN_HEAD_DIM ** -0.5 * LOG2E)
                    for c0 in range(0, d_attn, slab)]
                 + [functools.partial(out_slab, kv_ref, wb_ref, o_kv, c0, 1.0) for c0 in range(0, d_kv2, slab)])
    strips = [(k, c0) for c0 in range(0, d_xbc, LANES) for k in range(n_chunks)]
    per_task = -(-len(strips) // (len(mxu_tasks) - 4))
    dt_raw = jnp.dot(h, wb_ref[:, o_dt:o_dt + LANES], preferred_element_type=F32)
    decay_chunks = list(range(n_chunks))
    ready_cols = 0
    for task in mxu_tasks:
        task()
        ready_cols += slab * (task.func is xbc_slab)
        if decay_chunks:
            decay_terms(dt_raw, decay_chunks.pop(0))
        for _ in range(per_task):
            if strips and strips[0][1] < ready_cols:
                conv_strip(*strips.pop(0))
    for strip in strips:
        conv_strip(*strip)


def _in_proj(x2, gain, wa, wb, conv_w, conv_b, dt_bias, a_log, *, seq, tm, d_ssm, d_xbc, d_attn, d_kv2):
    t, d = x2.shape
    halo = SUBLANES
    nsub = tm // halo
    nhalo = t // halo
    chunks = tm // CHUNK
    gs = SSM_GROUPS * SSM_STATE
    row = lambda rows, width: pl.BlockSpec((rows, width), lambda i: (i, 0))
    out_shape = [
        jax.ShapeDtypeStruct((t, d_ssm), BF16),
        jax.ShapeDtypeStruct((t, d_ssm), BF16),
        jax.ShapeDtypeStruct((t // CHUNK * gs, CHUNK), BF16),
        jax.ShapeDtypeStruct((t, gs), BF16),
        jax.ShapeDtypeStruct((t, LANES), F32),
        jax.ShapeDtypeStruct((t // CHUNK * 4 * SSM_HEADS, CHUNK), F32),
        jax.ShapeDtypeStruct((t, d_attn), BF16),
        jax.ShapeDtypeStruct((t, d_kv2), BF16),
    ]
    return pl.pallas_call(
        functools.partial(_in_proj_body, tm=tm, blocks_per_seq=seq // tm, d_ssm=d_ssm, d_xbc=d_xbc,
                          d_attn=d_attn, d_kv2=d_kv2),
        grid=(t // tm,),
        in_specs=[row(tm, d),
                  pl.BlockSpec((halo, d), lambda i: (jnp.maximum(i * nsub - 1, 0), 0)),
                  pl.BlockSpec((halo, d), lambda i: (jnp.minimum((i + 1) * nsub, nhalo - 1), 0)),
                  _resident((1, d)),
                  pl.BlockSpec((d, d_ssm + d_xbc), lambda i: (0, 0), pipeline_mode=pl.Buffered(1)),
                  _resident(wb.shape), _resident((CONV_WIDTH, d_xbc)), _resident((1, d_xbc)),
                  _resident((2 * SSM_HEADS, CHUNK)), _resident((2 * SSM_HEADS, CHUNK))],
        out_specs=[row(tm, d_ssm), row(tm, d_ssm), row(chunks * gs, CHUNK), row(tm, gs), row(tm, LANES),
                   row(chunks * 4 * SSM_HEADS, CHUNK), row(tm, d_attn), row(tm, d_kv2)],
        out_shape=out_shape,
        scratch_shapes=[pltpu.VMEM((tm + 2 * halo + SUBLANES, d_xbc), F32)],
        compiler_params=_params("parallel"),
        name="in_proj",
    )(x2, x2, x2, gain, wa, wb, conv_w, conv_b, dt_bias, a_log)


SSD_CHUNKS_PER_STEP = 8


def _ssd_scan_body(xf_ref, btf_ref, ccf_ref, acf_ref, drf_ref, xb_ref, btb_ref, ccb_ref, acb_ref, drb_ref,
                   dskip_ref, yf_ref, yb_ref, stf_ref, stb_ref):
    @pl.when(pl.program_id(1) == 0)
    def _():
        stf_ref[...] = jnp.zeros_like(stf_ref)
        stb_ref[...] = jnp.zeros_like(stb_ref)

    gw = (SSM_HEADS // SSM_GROUPS) * SSM_HEAD_DIM
    pairs_per_group = gw // LANES
    row = lax.broadcasted_iota(jnp.int32, (CHUNK, CHUNK), 0)
    lane = lax.broadcasted_iota(jnp.int32, (CHUNK, CHUNK), 1)
    first_head = lane < SSM_HEAD_DIM
    keep_first = jnp.where(first_head, 1.0, 0.0).astype(BF16)
    keep_second = jnp.where(first_head, 0.0, 1.0).astype(BF16)

    gs = SSM_GROUPS * SSM_STATE

    def visit(x_ref, bt_ref, cc_ref, ac_ref, dr_ref, st_ref, y_ref, direction, k):
        off = direction * SSM_HEADS
        inside = row >= lane if direction == 0 else row <= lane
        edge = CHUNK - 1 if direction == 0 else 0
        rows = slice(k * CHUNK, (k + 1) * CHUNK)
        ac = ac_ref[rows, :]
        dr = dr_ref[k * 4 * SSM_HEADS:(k + 1) * 4 * SSM_HEADS, :]
        for g in range(SSM_GROUPS):
            btg = bt_ref[k * gs + g * SSM_STATE:k * gs + (g + 1) * SSM_STATE, :]
            cg = cc_ref[rows, g * SSM_STATE:(g + 1) * SSM_STATE]
            cb = jnp.dot(cg, btg, preferred_element_type=F32)
            cbm = jnp.where(inside, cb, 0.0).astype(BF16)
            btf = btg.astype(F32)
            st = st_ref[:, g * gw:(g + 1) * gw]
            y_off = jnp.dot(cg, st.astype(BF16), preferred_element_type=F32)
            for jj in range(pairs_per_group):
                j = g * pairs_per_group + jj
                sl = slice(j * LANES, (j + 1) * LANES)
                decays, weighted_bt, cols = [], [], []
                for h in (2 * j, 2 * j + 1):
                    col = jnp.broadcast_to(ac[:, off + h:off + h + 1], (CHUNK, CHUNK))
                    grow = dr[off + h:off + h + 1, :]
                    wrow = dr[2 * SSM_HEADS + off + h:2 * SSM_HEADS + off + h + 1, :]
                    decay = jnp.minimum(jnp.exp2(col - grow).astype(BF16), BF16_HUGE)
                    decays.append(decay * cbm)
                    weighted_bt.append((btf * wrow).astype(BF16))
                    cols.append(col)
                lhs = jnp.concatenate([jnp.concatenate(decays, axis=1), jnp.concatenate(weighted_bt, axis=1)], axis=0)
                xp = x_ref[rows, sl]
                rhs = jnp.concatenate([xp * keep_first, xp * keep_second], axis=0)
                out = jnp.dot(lhs, rhs, preferred_element_type=F32)
                ea = jnp.exp2(jnp.where(first_head, cols[0], cols[1]))
                y = out[0:CHUNK] + ea * y_off[:, jj * LANES:(jj + 1) * LANES]
                if direction == 0:
                    y = y + xp.astype(F32) * dskip_ref[:, sl]
                y_ref[rows, sl] = y.astype(BF16)
                st_ref[:, sl] = st[:, jj * LANES:(jj + 1) * LANES] * ea[edge:edge + 1, :] + out[CHUNK:]

    for k in range(SSD_CHUNKS_PER_STEP):
        visit(xf_ref, btf_ref, ccf_ref, acf_ref, drf_ref, stf_ref, yf_ref, 0, k)
        visit(xb_ref, btb_ref, ccb_ref, acb_ref, drb_ref, stb_ref, yb_ref, 1, SSD_CHUNKS_PER_STEP - 1 - k)


def _ssd_scan(xc, bt, cc, acol, drow, dskip, *, batch, seq):
    t, d_ssm = xc.shape
    per = SSD_CHUNKS_PER_STEP
    assert seq % (per * CHUNK) == 0
    ns = seq // (per * CHUNK)
    gs = SSM_GROUPS * SSM_STATE
    fwd = lambda rows, width: pl.BlockSpec((per * rows, width), lambda b, c: (b * ns + c, 0))
    bwd = lambda rows, width: pl.BlockSpec((per * rows, width), lambda b, c: (b * ns + ns - 1 - c, 0))
    operands = lambda spec: [spec(CHUNK, d_ssm), spec(gs, CHUNK), spec(CHUNK, gs), spec(CHUNK, LANES),
                             spec(4 * SSM_HEADS, CHUNK)]
    return pl.pallas_call(
        _ssd_scan_body,
        grid=(batch, ns),
        in_specs=operands(fwd) + operands(bwd) + [_resident((1, d_ssm))],
        out_specs=[fwd(CHUNK, d_ssm), bwd(CHUNK, d_ssm)],
        out_shape=[jax.ShapeDtypeStruct((t, d_ssm), BF16), jax.ShapeDtypeStruct((t, d_ssm), BF16)],
        scratch_shapes=[pltpu.VMEM((SSM_STATE, d_ssm), F32), pltpu.VMEM((SSM_STATE, d_ssm), F32)],
        compiler_params=_params("arbitrary", "arbitrary"),
        name="ssd_scan",
    )(xc, bt, cc, acol, drow, xc, bt, cc, acol, drow, dskip)


ATTN_BLOCKS_PER_STEP = 8


def _attn_body(q_ref, kvp_ref, kvc_ref, kvn_ref, bias_lo_ref, bias_mid_ref, bias_hi_ref, sink_ref, o_ref, s_ref):
    rep = ATTN_HEADS // ATTN_KV_HEADS
    d_kv = ATTN_KV_HEADS * ATTN_HEAD_DIM
    kv = jnp.concatenate([kvp_ref[...], kvc_ref[...], kvn_ref[...]], axis=0)
    bias_refs = (bias_lo_ref,) + (bias_mid_ref,) * (ATTN_BLOCKS_PER_STEP - 2) + (bias_hi_ref,)
    units = [(blk, g) for blk in range(ATTN_BLOCKS_PER_STEP) for g in range(ATTN_KV_HEADS)]
    row_max = {}
    for blk, g in units:
        rows = slice(blk * WINDOW, (blk + 1) * WINDOW)
        kg = kv[blk * WINDOW:(blk + 3) * WINDOW, g * ATTN_HEAD_DIM:(g + 1) * ATTN_HEAD_DIM]
        qg = jnp.concatenate([q_ref[rows, (g * rep + r) * ATTN_HEAD_DIM:(g * rep + r + 1) * ATTN_HEAD_DIM]
                              for r in range(rep)], axis=0)
        s = lax.dot_general(kg, qg, NT_DIMS, preferred_element_type=F32) + bias_refs[blk][0, g]
        s_ref[blk, g] = s
        row_max[blk, g] = jnp.maximum(jnp.max(s, axis=0, keepdims=True), sink_ref[g])
    for blk, g in units:
        rows = slice(blk * WINDOW, (blk + 1) * WINDOW)
        vg = kv[blk * WINDOW:(blk + 3) * WINDOW, d_kv + g * ATTN_HEAD_DIM:d_kv + (g + 1) * ATTN_HEAD_DIM]
        m = row_max[blk, g]
        p = jnp.exp2(s_ref[blk, g] - m)
        denom = jnp.sum(p, axis=0, keepdims=True) + jnp.exp2(sink_ref[g] - m)
        o = lax.dot_general(vg, p.astype(BF16), TN_DIMS, preferred_element_type=F32) / denom
        for r in range(0, rep, 2):
            two = jnp.concatenate([o[:, r * WINDOW:(r + 1) * WINDOW], o[:, (r + 1) * WINDOW:(r + 2) * WINDOW]], axis=0)
            lo = (g * rep + r) * ATTN_HEAD_DIM
            o_ref[rows, lo:lo + 2 * ATTN_HEAD_DIM] = two.T.astype(BF16)


def _attention(q, kv, bias, sink, *, batch, seq):
    t, d_attn = q.shape
    per = ATTN_BLOCKS_PER_STEP
    assert per >= 2 and seq % (per * WINDOW) == 0
    ns = seq // (per * WINDOW)
    nb = seq // WINDOW
    rep = ATTN_HEADS // ATTN_KV_HEADS
    edge = lambda f: pl.BlockSpec((WINDOW, kv.shape[1]), f)
    table = lambda f: pl.BlockSpec((1, ATTN_KV_HEADS, 3 * WINDOW, rep * WINDOW), f)
    return pl.pallas_call(
        _attn_body,
        grid=(batch, ns),
        in_specs=[
            pl.BlockSpec((per * WINDOW, d_attn), lambda b, n: (b * ns + n, 0)),
            edge(lambda b, n: (b * nb + jnp.maximum(per * n - 1, 0), 0)),
            pl.BlockSpec((per * WINDOW, kv.shape[1]), lambda b, n: (b * ns + n, 0)),
            edge(lambda b, n: (b * nb + jnp.minimum(per * n + per, nb - 1), 0)),
            table(lambda b, n: (jnp.where(n == 0, 0, 1), 0, 0, 0)),
            pl.BlockSpec((1, ATTN_KV_HEADS, 3 * WINDOW, rep * WINDOW), lambda b, n: (1, 0, 0, 0),
                         pipeline_mode=pl.Buffered(1)),
            table(lambda b, n: (jnp.where(n == ns - 1, 2, 1), 0, 0, 0)),
            _resident((ATTN_KV_HEADS, 1, rep * WINDOW)),
        ],
        out_specs=pl.BlockSpec((per * WINDOW, d_attn), lambda b, n: (b * ns + n, 0)),
        out_shape=jax.ShapeDtypeStruct((t, d_attn), BF16),
        scratch_shapes=[pltpu.VMEM((per, ATTN_KV_HEADS, 3 * WINDOW, rep * WINDOW), F32)],
        compiler_params=_params("parallel", "parallel"),
        name="attention",
    )(q, kv, kv, kv, bias, bias, bias, sink)


def _attn_bias_tables():
    rep = ATTN_HEADS // ATTN_KV_HEADS
    kj = np.arange(3 * WINDOW)[:, None]
    qi = np.arange(WINDOW)[None, :]
    dist = np.abs(kj - WINDOW - qi).astype(np.float32)
    in_window = dist <= WINDOW
    slopes = 2.0 ** (-8.0 * np.arange(1, ATTN_HEADS + 1, dtype=np.float32) / ATTN_HEADS)
    alibi = -slopes.reshape(ATTN_KV_HEADS, rep)[:, None, :, None] * dist[None, :, None, :]
    alibi = alibi * np.float32(LOG2E)
    cases = []
    for has_prev, has_next in ((False, True), (True, True), (True, False)):
        ok = in_window & ((kj >= WINDOW) | has_prev) & ((kj < 2 * WINDOW) | has_next)
        cases.append(np.where(ok[None, :, None, :], alibi, np.float32(MASKED)))
    out = np.stack(cases).astype(np.float32)
    return out.reshape(3, ATTN_KV_HEADS, 3 * WINDOW, rep * WINDOW)


def _out_mlp_body(x_ref, yf_ref, yb_ref, z_ref, ya_ref, nssm_ref, wo_ref, npost_ref, npre2_ref, wu_ref, wd_ref,
                  npost2_ref, o_ref, *, d_ssm):
    half = x_ref.shape[0] // 2
    parts = [slice(0, half), slice(half, 2 * half)]
    attn_mix = jnp.dot(ya_ref[...], wo_ref[d_ssm:, :], preferred_element_type=F32)
    ys = []
    for rows in parts:
        y = yf_ref[rows, :].astype(F32) + yb_ref[rows, :].astype(F32)
        ys.append(_rmsnorm(y * _silu(z_ref[rows, :].astype(F32)), nssm_ref[...]).astype(BF16))
    x1, h = [], []
    for rows, ysr in zip(parts, ys):
        mix = jnp.dot(ysr, wo_ref[:d_ssm, :], preferred_element_type=F32) + attn_mix[rows]
        x1.append(x_ref[rows, :] + _rmsnorm(mix, npost_ref[...]))
        h.append(_rmsnorm(x1[-1], npre2_ref[...]).astype(BF16))
    u = [jnp.maximum(jnp.dot(hr, wu_ref[...], preferred_element_type=F32), 0.0) for hr in h]
    f = [jnp.dot((ur * ur).astype(BF16), wd_ref[...], preferred_element_type=F32) for ur in u]
    for rows, x1r, fr in zip(parts, x1, f):
        o_ref[rows, :] = x1r + _rmsnorm(fr, npost2_ref[...])


def _out_mlp(x2, yf, yb, z, ya, nssm, wo, npost, npre2, wu, wd, npost2, *, tm):
    t, d = x2.shape
    d_ssm = yf.shape[1]
    row = lambda width: pl.BlockSpec((tm, width), lambda i: (i, 0))
    return pl.pallas_call(
        functools.partial(_out_mlp_body, d_ssm=d_ssm),
        grid=(t // tm,),
        in_specs=[row(d), row(d_ssm), row(d_ssm), row(d_ssm), row(ya.shape[1]),
                  _resident((1, d_ssm)), _resident(wo.shape), _resident((1, d)),
                  _resident((1, d)), _resident(wu.shape), _resident(wd.shape), _resident((1, d))],
        out_specs=row(d),
        out_shape=jax.ShapeDtypeStruct((t, d), F32),
        compiler_params=_params("parallel"),
        name="out_mlp",
    )(x2, yf, yb, z, ya, nssm, wo, npost, npre2, wu, wd, npost2)


def _layer(x2, p, *, batch, seq):
    d = x2.shape[1]
    d_ssm = SSM_HEADS * SSM_HEAD_DIM
    d_xbc = d_ssm + 2 * SSM_GROUPS * SSM_STATE
    d_attn = ATTN_HEADS * ATTN_HEAD_DIM
    d_kv2 = 2 * ATTN_KV_HEADS * ATTN_HEAD_DIM
    n_dt = 2 * SSM_HEADS
    row = lambda v: v.reshape(1, -1).astype(F32)
    per_stream = lambda v: jnp.broadcast_to(v.reshape(n_dt, 1).astype(F32), (n_dt, CHUNK))

    wa = p["w_in"].astype(BF16)
    o_dt = d_ssm + d_xbc
    wb = jnp.concatenate([wa[:, o_dt + n_dt:], wa[:, o_dt:o_dt + n_dt], jnp.zeros((d, LANES - n_dt), BF16)], axis=1)

    z, xc, bt, cc, acol, drow, q, kv = _in_proj(
        x2, row(p["norm_mix_pre"]), wa, wb, p["conv_w"].astype(F32), row(p["conv_b"]),
        per_stream(p["dt_bias"]), per_stream(p["a_log"]),
        seq=seq, tm=512, d_ssm=d_ssm, d_xbc=d_xbc, d_attn=d_attn, d_kv2=d_kv2)
    dskip = jnp.repeat(p["d_skip"].astype(F32), SSM_HEAD_DIM).reshape(1, d_ssm)
    yf, yb = _ssd_scan(xc, bt, cc, acol, drow, dskip, batch=batch, seq=seq)

    rep = ATTN_HEADS // ATTN_KV_HEADS
    sink = jnp.repeat(p["attn_sink"].astype(F32) * LOG2E, WINDOW).reshape(ATTN_KV_HEADS, 1, rep * WINDOW)
    ya = _attention(q, kv, jnp.asarray(_attn_bias_tables()), sink, batch=batch, seq=seq)

    return _out_mlp(x2, yf, yb, z, ya, row(p["ssm_norm"]), p["w_out"].astype(BF16), row(p["norm_mix_post"]),
                    row(p["norm_mlp_pre"]), p["w_up"].astype(BF16), p["w_down"].astype(BF16),
                    row(p["norm_mlp_post"]), tm=512)


def kernel(x, norm_mix_pre, w_in, conv_w, conv_b, dt_bias, a_log, d_skip, ssm_norm, attn_sink, w_out,
           norm_mix_post, norm_mlp_pre, w_up, w_down, norm_mlp_post):
    batch, seq, d = x.shape
    names = ("norm_mix_pre", "w_in", "conv_w", "conv_b", "dt_bias", "a_log", "d_skip", "ssm_norm", "attn_sink",
             "w_out", "norm_mix_post", "norm_mlp_pre", "w_up", "w_down", "norm_mlp_post")
    stacked = (norm_mix_pre, w_in, conv_w, conv_b, dt_bias, a_log, d_skip, ssm_norm, attn_sink, w_out,
               norm_mix_post, norm_mlp_pre, w_up, w_down, norm_mlp_post)
    x2 = x.reshape(batch * seq, d)
    for i in range(w_in.shape[0]):
        x2 = _layer(x2, {k: v[i] for k, v in zip(names, stacked)}, batch=batch, seq=seq)
    return x2.reshape(batch, seq, d)
```
